```python
import jax
import jax.numpy as jnp
from jax import lax
import numpy as np

D_MODEL = 1024
BATCH = 4
SEQ = 4096
DEPTH = 4
DEC_BATCH = 32
DEC_SEQ = 4
PAST_LEN = 8192
PAGE_SIZE = 128

N_REC = DEPTH // 2
N_ATT = DEPTH - N_REC
A_HEADS = 4
A_DV = 128
A_DK = A_DV // 2
A_WIDTH = A_HEADS * A_DV
A_QK_W = A_HEADS * A_DK
MLSTM_CHUNK = 128
F_BIAS = 3.0
B_GROUPS = 4
B_CHUNK = 128
B_WIDTH = D_MODEL // 2
B_GROUP_DIM = B_WIDTH // B_GROUPS
C_HEADS = 8
C_HD = D_MODEL // C_HEADS
C_WIDTH = C_HEADS * C_HD
BLOCK_Q = 128
SB_SCALE = C_HD ** -0.5
SB_BIAS_INIT = -9.0
D_FF = 4 * D_MODEL
N_MOD = 6
EPS = 1e-6
AB_SPLITS = [A_QK_W, 2 * A_QK_W, 2 * A_QK_W + A_WIDTH, 2 * A_QK_W + 2 * A_WIDTH,
             2 * A_QK_W + 2 * A_WIDTH + B_WIDTH, 2 * A_QK_W + 2 * A_WIDTH + 2 * B_WIDTH]
AB_IN = AB_SPLITS[-1] + 2 * A_HEADS

kernel_name = 'hybrid_mlstm_gmlp_stickbreaking_step'

F32 = jnp.float32


def rmsnorm(x, w):
    xf = x.astype(F32)
    y = xf * lax.rsqrt(jnp.mean(xf * xf, axis=-1, keepdims=True) + EPS)
    return (y * w.astype(F32)).astype(x.dtype)


def modulate(x, w, shift, scale):
    return rmsnorm(x, w) * (1.0 + scale[:, None, :]) + shift[:, None, :]


def mlstm_chunk(state, inp):
    C, n, m = state
    q, k, v, ig, lf = inp
    L = q.shape[2]
    causal = jnp.tril(jnp.ones((L, L), dtype=bool))
    b = jnp.cumsum(lf, axis=-1)
    d = jnp.where(causal, b[..., :, None] - b[..., None, :] + ig[..., None, :], -jnp.inf)
    inter = b + m[..., None]
    m_t = jnp.maximum(inter, jnp.max(d, axis=-1))
    w_st = jnp.exp(inter - m_t)
    p = jnp.einsum('bhtd,bhsd->bhts', q, k) * jnp.exp(d - m_t[..., None])
    num = jnp.einsum('bhts,bhsv->bhtv', p, v) + w_st[..., None] * jnp.einsum('bhtd,bhdv->bhtv', q, C)
    den = jnp.sum(p, axis=-1) + w_st * jnp.einsum('bhtd,bhd->bht', q, n)
    h = num / jnp.maximum(jnp.abs(den), jnp.exp(-m_t))[..., None]
    b_end = b[..., -1:]
    m_new = m_t[..., -1]
    w_k = jnp.exp(b_end - b + ig - m_new[..., None])
    decay = jnp.exp(b_end[..., 0] + m - m_new)
    C_new = decay[..., None, None] * C + jnp.einsum('bhs,bhsd,bhsv->bhdv', w_k, k, v)
    n_new = decay[..., None] * n + jnp.einsum('bhs,bhsd->bhd', w_k, k)
    return (C_new, n_new, m_new), h


def mlstm_scan(q, k, v, ig, lf, state, chunk):
    B, T = q.shape[:2]
    nc = T // chunk

    def chunks(a):
        a = a.astype(F32).reshape((B, nc, chunk) + a.shape[2:])
        return jnp.swapaxes(jnp.moveaxis(a, 1, 0), 2, 3)

    state, h = lax.scan(mlstm_chunk, state, (chunks(q), chunks(k), chunks(v), chunks(ig), chunks(lf)))
    h = jnp.moveaxis(jnp.swapaxes(h, 2, 3), 0, 1).reshape(B, T, A_HEADS, A_DV)
    return h, state


def spatial_gate(u, vs, w_s, b_s, chunk):
    B, T = u.shape[:2]
    nc = T // chunk
    w = jnp.tril(w_s[:, :chunk, :chunk])
    vc = vs.reshape(B, nc, chunk, B_GROUPS, B_GROUP_DIM)
    s = jnp.einsum('gts,bcsgd->bctgd', w, vc) + jnp.transpose(b_s[:, :chunk])[None, None, :, :, None]
    return u * s.reshape(B, T, B_WIDTH)


def stick_breaking_weights(z, q_pos, k_pos):
    mask = k_pos[None, :] < q_pos[:, None]
    log_1mb = jnp.where(mask, jax.nn.log_sigmoid(-z), 0.0)
    rest = lax.cumsum(log_1mb, axis=3, reverse=True) - log_1mb
    return jnp.where(mask, jnp.exp(jax.nn.log_sigmoid(z) + rest), 0.0)


def even_mixer(h, e, W, state, chunk_a, chunk_b):
    B, T, _ = h.shape
    p = h @ W['w_in_ab'][e]
    q, k, v, o, u, vs, g = jnp.split(p, AB_SPLITS, axis=-1)
    g = g.astype(F32) + W['b_if'][e].astype(F32)
    ig = g[..., :A_HEADS]
    lf = jax.nn.log_sigmoid(g[..., A_HEADS:])
    q = q.reshape(B, T, A_HEADS, A_DK)
    k = k.reshape(B, T, A_HEADS, A_DK) * (A_DK ** -0.5)
    v = v.reshape(B, T, A_HEADS, A_DV)
    ha, state = mlstm_scan(q, k, v, ig, lf, state, chunk_a)
    ha = rmsnorm(ha, W['mh_norm_w'][e].reshape(A_HEADS, A_DV))
    ha = ha.reshape(B, T, A_WIDTH).astype(h.dtype) * jax.nn.sigmoid(o)
    u = jax.nn.gelu(u)
    vs = rmsnorm(jax.nn.gelu(vs).reshape(B, T, B_GROUPS, B_GROUP_DIM),
                 W['sg_norm_w'][e].reshape(B_GROUPS, B_GROUP_DIM))
    zb = spatial_gate(u, vs, W['sg_w'][e], W['sg_b'][e], chunk_b)
    out = jnp.concatenate([ha, zb], axis=-1) @ W['w_out_ab'][e]
    return out, state, vs.reshape(B, T, B_WIDTH)


def odd_mixer(h, o, W, past, pos0):
    B, T, _ = h.shape
    q, k, v = jnp.split(h @ W['w_qkv_c'][o], 3, axis=-1)
    q = q.reshape(B, T, C_HEADS, C_HD)
    k = k.reshape(B, T, C_HEADS, C_HD)
    v = v.reshape(B, T, C_HEADS, C_HD)
    bias = W['b_sb'][o].astype(F32)[None, :, None, None]
    q_pos = pos0 + jnp.arange(T)
    if past is None:
        k_pos = jnp.arange(T)
        nb = T // BLOCK_Q
        qb = jnp.moveaxis(q.reshape(B, nb, BLOCK_Q, C_HEADS, C_HD), 1, 0)
        pb = q_pos.reshape(nb, BLOCK_Q)

        def block(args):
            qi, pi = args
            z = jnp.einsum('bthd,bshd->bhts', qi, k, preferred_element_type=F32) * SB_SCALE + bias
            a = stick_breaking_weights(z, pi, k_pos)
            return jnp.einsum('bhts,bshd->bthd', a.astype(v.dtype), v, preferred_element_type=F32)

        att = jnp.moveaxis(lax.map(block, (qb, pb)), 0, 1).reshape(B, T, C_WIDTH)
    else:
        k_past, v_past = past
        P = k_past.shape[1]
        k_pos = jnp.arange(P + T)
        z = jnp.concatenate([
            jnp.einsum('bthd,bshd->bhts', q, k_past, preferred_element_type=F32),
            jnp.einsum('bthd,bshd->bhts', q, k, preferred_element_type=F32)], axis=3) * SB_SCALE + bias
        a = stick_breaking_weights(z, q_pos, k_pos).astype(v.dtype)
        att = (jnp.einsum('bhts,bshd->bthd', a[..., :P], v_past, preferred_element_type=F32)
               + jnp.einsum('bhts,bshd->bthd', a[..., P:], v, preferred_element_type=F32)).reshape(B, T, C_WIDTH)
    out = att.astype(h.dtype) @ W['w_out_c'][o]
    return out, k, v


def sqrelu_mlp(h, w_up, w_down):
    return jnp.square(jax.nn.relu(h @ w_up)) @ w_down


def trunk(x, c, W, rec_state, kv_past, chunk_a, chunk_b, pos0):
    silu_c = jax.nn.silu(c)
    rec_C, rec_n, rec_m, rows, ks, vs = [], [], [], [], [], []
    for l in range(DEPTH):
        mod = silu_c @ W['w_ada'][l] + W['b_ada'][l]
        sh1, sc1, g1, sh2, sc2, g2 = jnp.split(mod, N_MOD, axis=-1)
        h = modulate(x, W['norm_w'][l, 0], sh1, sc1)
        if l % 2 == 0:
            e = l // 2
            out, (Cn, nn, mn), r = even_mixer(h, e, W, rec_state(e), chunk_a, chunk_b)
            rec_C.append(Cn)
            rec_n.append(nn)
            rec_m.append(mn)
            rows.append(r)
        else:
            o = l // 2
            out, kn, vn = odd_mixer(h, o, W, kv_past(o), pos0)
            ks.append(kn)
            vs.append(vn)
        x = x + g1[:, None, :] * out
        h = modulate(x, W['norm_w'][l, 1], sh2, sc2)
        x = x + g2[:, None, :] * sqrelu_mlp(h, W['w_up'][l], W['w_down'][l])
    y = rmsnorm(x, W['final_norm_w'])
    return y, jnp.stack(rec_C), jnp.stack(rec_n), jnp.stack(rec_m), jnp.stack(rows), jnp.stack(ks), jnp.stack(vs)


def setup_inputs(seed: int = 0) -> dict:
    key = jax.random.key(seed)
    ks = jax.random.split(key, 32)
    nrm = lambda k, shape, s=1.0: s * jax.random.normal(k, shape, F32)
    n_pages = PAST_LEN // PAGE_SIZE
    n_used = DEC_BATCH * n_pages
    n_phys = n_used + n_used // 4
    page_table = jax.random.permutation(ks[0], n_phys)[:n_used].reshape(DEC_BATCH, n_pages).astype(jnp.int32)
    b_if = jnp.concatenate([nrm(ks[1], (N_REC, A_HEADS), 0.1),
                            F_BIAS + nrm(ks[2], (N_REC, A_HEADS), 0.1)], axis=-1)
    return {
        'x_prompt': nrm(ks[3], (BATCH, SEQ, D_MODEL)),
        'x_sample': nrm(ks[4], (DEC_BATCH, DEC_SEQ, D_MODEL)),
        'state_mlstm_C': nrm(ks[5], (N_REC, DEC_BATCH, A_HEADS, A_DK, A_DV), 0.3),
        'state_mlstm_n': nrm(ks[6], (N_REC, DEC_BATCH, A_HEADS, A_DK), 0.3),
        'state_mlstm_m': nrm(ks[7], (N_REC, DEC_BATCH, A_HEADS)),
        'cache_k': nrm(ks[8], (n_phys, N_ATT, PAGE_SIZE, C_HEADS, C_HD)),
        'cache_v': nrm(ks[9], (n_phys, N_ATT, PAGE_SIZE, C_HEADS, C_HD)),
        'page_table': page_table,
        'c_prompt': nrm(ks[10], (BATCH, D_MODEL)),
        'c_sample': nrm(ks[11], (DEC_BATCH, D_MODEL)),
        'w_ada': nrm(ks[12], (DEPTH, D_MODEL, N_MOD * D_MODEL), 0.5 * D_MODEL ** -0.5),
        'b_ada': nrm(ks[13], (DEPTH, N_MOD * D_MODEL), 0.02),
        'norm_w': 1.0 + nrm(ks[14], (DEPTH, 2, D_MODEL), 0.02),
        'final_norm_w': 1.0 + nrm(ks[15], (D_MODEL,), 0.02),
        'w_in_ab': nrm(ks[16], (N_REC, D_MODEL, AB_IN), D_MODEL ** -0.5),
        'b_if': b_if,
        'mh_norm_w': 1.0 + nrm(ks[17], (N_REC, A_WIDTH), 0.02),
        'sg_norm_w': 1.0 + nrm(ks[18], (N_REC, B_WIDTH), 0.02),
        'sg_w': nrm(ks[19], (N_REC, B_GROUPS, B_CHUNK, B_CHUNK), B_CHUNK ** -0.5),
        'sg_b': 1.0 + nrm(ks[20], (N_REC, B_GROUPS, B_CHUNK), 0.1),
        'w_out_ab': nrm(ks[21], (N_REC, A_WIDTH + B_WIDTH, D_MODEL), (A_WIDTH + B_WIDTH) ** -0.5),
        'w_qkv_c': nrm(ks[22], (N_ATT, D_MODEL, 3 * C_WIDTH), D_MODEL ** -0.5),
        'b_sb': SB_BIAS_INIT + nrm(ks[26], (N_ATT, C_HEADS), 0.3),
        'w_out_c': nrm(ks[23], (N_ATT, C_WIDTH, D_MODEL), C_WIDTH ** -0.5),
        'w_up': nrm(ks[24], (DEPTH, D_MODEL, D_FF), D_MODEL ** -0.5),
        'w_down': nrm(ks[25], (DEPTH, D_FF, D_MODEL), D_FF ** -0.5),
    }


def reference(x_prompt, x_sample, state_mlstm_C, state_mlstm_n, state_mlstm_m, cache_k, cache_v, page_table,
              c_prompt, c_sample, w_ada, b_ada, norm_w, final_norm_w, w_in_ab, b_if, mh_norm_w, sg_norm_w,
              sg_w, sg_b, w_out_ab, w_qkv_c, b_sb, w_out_c, w_up, w_down):
    W = dict(w_ada=w_ada, b_ada=b_ada, norm_w=norm_w, final_norm_w=final_norm_w, w_in_ab=w_in_ab, b_if=b_if,
             mh_norm_w=mh_norm_w, sg_norm_w=sg_norm_w, sg_w=sg_w, sg_b=sg_b, w_out_ab=w_out_ab,
             w_qkv_c=w_qkv_c, b_sb=b_sb, w_out_c=w_out_c, w_up=w_up, w_down=w_down)
    bp = x_prompt.shape[0]
    init = (jnp.zeros((bp, A_HEADS, A_DK, A_DV), F32), jnp.zeros((bp, A_HEADS, A_DK), F32),
            jnp.zeros((bp, A_HEADS), F32))
    y_prompt, pC, pn, pm, _, pk, pv = trunk(x_prompt, c_prompt, W, lambda e: init, lambda o: None,
                                           MLSTM_CHUNK, B_CHUNK, 0)

    def rec_state(e):
        return (state_mlstm_C[e].astype(F32), state_mlstm_n[e].astype(F32), state_mlstm_m[e].astype(F32))

    def kv_past(o):
        db = page_table.shape[0]
        kp = cache_k[page_table, o].reshape(db, -1, C_HEADS, C_HD)
        vp = cache_v[page_table, o].reshape(db, -1, C_HEADS, C_HD)
        return (kp, vp)

    t_s = x_sample.shape[1]
    y_sample, sC, sn, sm, s_rows, sk, sv = trunk(x_sample, c_sample, W, rec_state, kv_past, t_s, t_s, PAST_LEN)
    return (y_prompt, y_sample, pC, pn, pm, sC, sn, sm, s_rows, pk, pv, sk, sv)
```

```python
import functools

import jax
import jax.numpy as jnp
from jax import lax
from jax.experimental import pallas as pl
from jax.experimental.pallas import tpu as pltpu

F32 = jnp.float32
BF16 = jnp.bfloat16

D_MODEL = 1024
A_HEADS = 4
A_DV = 128
A_DK = 64
A_WIDTH = A_HEADS * A_DV
A_QK_W = A_HEADS * A_DK
B_GROUPS = 4
B_WIDTH = 512
B_GROUP_DIM = 128
C_HEADS = 8
C_HD = 128
C_WIDTH = C_HEADS * C_HD
SB_SCALE = C_HD ** -0.5
D_FF = 4 * D_MODEL
N_MOD = 6
EPS = 1e-6
AB_MAIN = 2 * A_QK_W + 2 * A_WIDTH + 2 * B_WIDTH
MIX_CHUNK = 128
ATT_TK = 128
SAMPLE_ROWS = 8
NEG = -1e30
LANES = 128
VMEM_LIMIT_V7X = 48 * 1024 * 1024


def _mm(a, b):
    return jnp.dot(a, b, preferred_element_type=F32)


def _nt(a, b):
    return lax.dot_general(a, b, (((1,), (1,)), ((), ())), preferred_element_type=F32)


def _tn(a, b):
    return lax.dot_general(a, b, (((0,), (0,)), ((), ())), preferred_element_type=F32)


def _softplus(z):
    return jnp.maximum(z, 0.0) + jnp.log1p(jnp.exp(-jnp.abs(z)))


def _rms(x, w):
    return x * lax.rsqrt(jnp.mean(x * x, axis=-1, keepdims=True) + EPS) * w


def _rms_mod(x, w, shift, scale):
    return _rms(x, w) * (1.0 + scale) + shift


def _split3(x):
    p1 = x.astype(BF16)
    r1 = x - p1.astype(F32)
    p2 = r1.astype(BF16)
    p3 = (r1 - p2.astype(F32)).astype(BF16)
    return p1, p2, p3


def _params(*sem):
    return pltpu.CompilerParams(dimension_semantics=sem, vmem_limit_bytes=VMEM_LIMIT_V7X)


def _mod_spec(arr, tm, rows_per_group):
    g, r, d = arr.shape
    assert r == 1 or r == tm
    bpg = rows_per_group // tm if r == 1 else 1
    return pl.BlockSpec((None, r, d), lambda i, *_: (i // bpg, 0, 0))


def _ada_kernel(c_ref, w_ref, b_ref, o_ref):
    c = c_ref[...]
    s = (c * jax.nn.sigmoid(c)).astype(BF16)
    o_ref[...] = _mm(s, w_ref[...].astype(BF16)) + b_ref[...]


def _ada(c_all, w_ada, b_ada):
    depth, d, n = w_ada.shape
    rows = c_all.shape[0]
    tn = 1536
    return pl.pallas_call(
        _ada_kernel,
        grid=(depth, n // tn),
        in_specs=[pl.BlockSpec((rows, d), lambda l, j: (0, 0)),
                  pl.BlockSpec((None, d, tn), lambda l, j: (l, 0, j)),
                  pl.BlockSpec((None, 1, tn), lambda l, j: (l, 0, j))],
        out_specs=pl.BlockSpec((None, rows, tn), lambda l, j: (l, 0, j)),
        out_shape=jax.ShapeDtypeStruct((depth, rows, n), F32),
        compiler_params=_params("parallel", "parallel"),
        name="ada_mod",
    )(c_all, w_ada, b_ada.reshape(depth, 1, n))


def _proj_kernel(x_ref, nw_ref, sh_ref, sc_ref, w_ref, *o_refs, splits):
    h = _rms_mod(x_ref[...], nw_ref[...], sh_ref[...], sc_ref[...]).astype(BF16)
    for o_ref, (a, b) in zip(o_refs, splits):
        o_ref[...] = _mm(h, w_ref[:, a:b]).astype(o_ref.dtype)


def _proj(x, nw, shift, scale, w, outs, tm, rows_per_group, name):
    n, d = x.shape
    splits = tuple((a, b) for a, b, _ in outs)
    return pl.pallas_call(
        functools.partial(_proj_kernel, splits=splits),
        grid=(n // tm,),
        in_specs=[pl.BlockSpec((tm, d), lambda i: (i, 0)),
                  pl.BlockSpec((1, d), lambda i: (0, 0)),
                  _mod_spec(shift, tm, rows_per_group),
                  _mod_spec(scale, tm, rows_per_group),
                  pl.BlockSpec(w.shape, lambda i: (0, 0))],
        out_specs=[pl.BlockSpec((tm, b - a), lambda i: (i, 0)) for a, b, _ in outs],
        out_shape=[jax.ShapeDtypeStruct((n, b - a), dt) for a, b, dt in outs],
        compiler_params=_params("parallel"),
        name=name,
    )(x, nw.reshape(1, d), shift, scale, w)


def _outproj_kernel(a_ref, w_ref, x_ref, g_ref, o_ref):
    o_ref[...] = x_ref[...] + g_ref[...] * _mm(a_ref[...].astype(BF16), w_ref[...])


def _outproj(a, w, x, gate, tm, rows_per_group, name):
    n, d = x.shape
    k = a.shape[1]
    return pl.pallas_call(
        _outproj_kernel,
        grid=(n // tm,),
        in_specs=[pl.BlockSpec((tm, k), lambda i: (i, 0)),
                  pl.BlockSpec(w.shape, lambda i: (0, 0)),
                  pl.BlockSpec((tm, d), lambda i: (i, 0)),
                  _mod_spec(gate, tm, rows_per_group)],
        out_specs=pl.BlockSpec((tm, d), lambda i: (i, 0)),
        out_shape=jax.ShapeDtypeStruct((n, d), F32),
        compiler_params=_params("parallel"),
        name=name,
    )(a, w, x, gate)


def _mlp_kernel(x_ref, nw_ref, sh_ref, sc_ref, g_ref, wu_ref, wd_ref, fnw_ref, o_ref, h_sc, acc_sc, *, final):
    f = pl.program_id(1)

    @pl.when(f == 0)
    def _():
        h_sc[...] = _rms_mod(x_ref[...], nw_ref[...], sh_ref[...], sc_ref[...]).astype(BF16)
        acc_sc[...] = jnp.zeros_like(acc_sc)

    up = _mm(h_sc[...], wu_ref[...])
    a = jnp.square(jnp.maximum(up, 0.0)).astype(BF16)
    acc_sc[...] += _mm(a, wd_ref[...])

    @pl.when(f == pl.num_programs(1) - 1)
    def _():
        y = x_ref[...] + g_ref[...] * acc_sc[...]
        if final:
            y = _rms(y, fnw_ref[...])
        o_ref[...] = y


def _mlp(x, nw, shift, scale, gate, w_up, w_down, fnw, final, tm, tf, rows_per_group, name):
    n, d = x.shape
    dff = w_up.shape[1]
    return pl.pallas_call(
        functools.partial(_mlp_kernel, final=final),
        grid=(n // tm, dff // tf),
        in_specs=[pl.BlockSpec((tm, d), lambda i, f: (i, 0)),
                  pl.BlockSpec((1, d), lambda i, f: (0, 0)),
                  _mod_spec(shift, tm, rows_per_group),
                  _mod_spec(scale, tm, rows_per_group),
                  _mod_spec(gate, tm, rows_per_group),
                  pl.BlockSpec((d, tf), lambda i, f: (0, f)),
                  pl.BlockSpec((tf, d), lambda i, f: (f, 0)),
                  pl.BlockSpec((1, d), lambda i, f: (0, 0))],
        out_specs=pl.BlockSpec((tm, d), lambda i, f: (i, 0)),
        out_shape=jax.ShapeDtypeStruct((n, d), F32),
        scratch_shapes=[pltpu.VMEM((tm, d), BF16), pltpu.VMEM((tm, d), F32)],
        compiler_params=_params("parallel", "arbitrary"),
        name=name,
    )(x, nw.reshape(1, d), shift, scale, gate, w_up, w_down, fnw.reshape(1, d))


def _mix_kernel(p_ref, g_ref, bif_ref, c0_ref, m0_ref, mhw_ref, sgnw_ref, sgw_ref, sgb_ref,
                hz_ref, rows_ref, cfin_ref, mfin_ref, c_sc, m_sc, *, block_rows, valid_len):
    L = MIX_CHUNK
    c = pl.program_id(1)

    @pl.when(c == 0)
    def _():
        c_sc[...] = c0_ref[...]
        m_sc[...] = m0_ref[...]

    p = p_ref[...]
    g = g_ref[...]
    if block_rows < L:
        p = jnp.concatenate([p, jnp.zeros((L - block_rows, p.shape[1]), F32)], axis=0)
        g = jnp.concatenate([g, jnp.zeros((L - block_rows, g.shape[1]), F32)], axis=0)

    rowi = lax.broadcasted_iota(jnp.int32, (L, LANES), 0)
    lanei = lax.broadcasted_iota(jnp.int32, (L, LANES), 1)
    is_i = lanei < A_HEADS
    is_f = (lanei >= A_HEADS) & (lanei < 2 * A_HEADS)
    gb = g + bif_ref[...]
    x8 = jnp.where(is_i, gb, jnp.where(is_f, -_softplus(-gb), 0.0))
    if valid_len < L:
        x8 = jnp.where(rowi < valid_len, x8, jnp.where(is_i, NEG, 0.0))

    sq_r = lax.broadcasted_iota(jnp.int32, (L, L), 0)
    sq_c = lax.broadcasted_iota(jnp.int32, (L, L), 1)
    causal = sq_c <= sq_r
    tril = jnp.where(causal, 1.0, 0.0).astype(BF16)
    eye = jnp.where(sq_c == sq_r, 1.0, 0.0).astype(BF16)

    parts = _split3(x8)
    bc = _mm(tril, parts[0]) + _mm(tril, parts[1]) + _mm(tril, parts[2])
    parts_t = [_nt(eye, q).astype(BF16) for q in parts]
    x8_t = parts_t[0].astype(F32) + parts_t[1].astype(F32) + parts_t[2].astype(F32)
    bc_t = _nt(parts_t[0], tril) + _nt(parts_t[1], tril) + _nt(parts_t[2], tril)

    lane_l = lax.broadcasted_iota(jnp.int32, (L, LANES), 1)
    e1 = jnp.where(lane_l == 0, 1.0, 0.0)
    m_row = m_sc[...]
    m_row_new = m_row
    has = []
    for h in range(A_HEADS):
        hp, hl = h // 2, h % 2
        headmask = (lane_l >= A_DK * hl) & (lane_l < A_DK * (hl + 1))
        q2 = p[:, hp * LANES:(hp + 1) * LANES]
        k2 = p[:, A_QK_W + hp * LANES:A_QK_W + (hp + 1) * LANES]
        v = p[:, 2 * A_QK_W + h * A_DV:2 * A_QK_W + (h + 1) * A_DV]
        qh = jnp.where(headmask, q2, 0.0).astype(BF16)
        kh = jnp.where(headmask, k2 * (A_DK ** -0.5), 0.0)
        ig_row = x8_t[h:h + 1, :]
        b_row = bc_t[A_HEADS + h:A_HEADS + h + 1, :]
        ig_col = x8[:, h:h + 1]
        b_col = bc[:, A_HEADS + h:A_HEADS + h + 1]
        m_prev = m_row[:, A_HEADS + h:A_HEADS + h + 1]

        d = jnp.where(causal, b_col - b_row + ig_row, NEG)
        inter = b_col + m_prev
        m_t = jnp.maximum(inter, jnp.max(d, axis=1, keepdims=True))
        w_st = jnp.exp(inter - m_t)
        pmat = _nt(qh, kh.astype(BF16)) * jnp.exp(d - m_t)
        v_ext = jnp.concatenate([v, e1], axis=1).astype(BF16)
        c_old = c_sc[h]
        num_ext = _mm(pmat.astype(BF16), v_ext) + w_st * _mm(qh, c_old.astype(BF16))
        den = num_ext[:, A_DV:A_DV + 1]
        hh = num_ext[:, :A_DV] / jnp.maximum(jnp.abs(den), jnp.exp(-m_t))

        m_new = m_t[L - 1:L, :]
        b_end = b_col[L - 1:L, :]
        w_k = jnp.exp(b_end - b_col + ig_col - m_new)
        decay = jnp.exp(b_end + m_prev - m_new)
        c_sc[h] = decay * c_old + _tn((kh * w_k).astype(BF16), v_ext)
        m_row_new = jnp.where(lax.broadcasted_iota(jnp.int32, m_row.shape, 1) == A_HEADS + h, m_new, m_row_new)

        o = p[:, 2 * A_QK_W + A_WIDTH + h * A_DV:2 * A_QK_W + A_WIDTH + (h + 1) * A_DV]
        has.append(_rms(hh, mhw_ref[:, h * A_DV:(h + 1) * A_DV]) * jax.nn.sigmoid(o))
    m_sc[...] = m_row_new

    u0 = 2 * A_QK_W + 2 * A_WIDTH
    zbs, vsn = [], []
    for gi in range(B_GROUPS):
        u = jax.nn.gelu(p[:, u0 + gi * B_GROUP_DIM:u0 + (gi + 1) * B_GROUP_DIM])
        vs = jax.nn.gelu(p[:, u0 + B_WIDTH + gi * B_GROUP_DIM:u0 + B_WIDTH + (gi + 1) * B_GROUP_DIM])
        vs = _rms(vs, sgnw_ref[:, gi * B_GROUP_DIM:(gi + 1) * B_GROUP_DIM])
        s = _mm(sgw_ref[gi], vs.astype(BF16)) + sgb_ref[:, gi:gi + 1]
        zbs.append(u * s)
        vsn.append(vs)

    hz = jnp.concatenate(has + zbs, axis=1)
    hz_ref[...] = hz[:block_rows].astype(hz_ref.dtype)
    rows_ref[...] = jnp.concatenate(vsn, axis=1)[:block_rows]

    @pl.when(c == pl.num_programs(1) - 1)
    def _():
        cfin_ref[...] = c_sc[...]
        mfin_ref[...] = m_sc[...]


def _mix(p, g, b_if, c0, m0, mh_norm_w, sg_norm_w, sg_w_tril, sg_b_t, n_seq, block_rows, valid_len, name):
    n = p.shape[0]
    nc = n // (n_seq * block_rows)
    row_map = lambda b, c: (b * nc + c, 0)
    const2 = lambda b, c: (0, 0)
    return pl.pallas_call(
        functools.partial(_mix_kernel, block_rows=block_rows, valid_len=valid_len),
        grid=(n_seq, nc),
        in_specs=[pl.BlockSpec((block_rows, AB_MAIN), row_map),
                  pl.BlockSpec((block_rows, LANES), row_map),
                  pl.BlockSpec((1, LANES), const2),
                  pl.BlockSpec((None, A_HEADS, LANES, 2 * LANES), lambda b, c: (b, 0, 0, 0)),
                  pl.BlockSpec((None, 1, LANES), lambda b, c: (b, 0, 0)),
                  pl.BlockSpec((1, A_WIDTH), const2),
                  pl.BlockSpec((1, B_WIDTH), const2),
                  pl.BlockSpec((B_GROUPS, MIX_CHUNK, MIX_CHUNK), lambda b, c: (0, 0, 0)),
                  pl.BlockSpec((MIX_CHUNK, LANES), const2)],
        out_specs=[pl.BlockSpec((block_rows, A_WIDTH + B_WIDTH), row_map),
                   pl.BlockSpec((block_rows, B_WIDTH), row_map),
                   pl.BlockSpec((None, A_HEADS, LANES, 2 * LANES), lambda b, c: (b, 0, 0, 0)),
                   pl.BlockSpec((None, 1, LANES), lambda b, c: (b, 0, 0))],
        out_shape=[jax.ShapeDtypeStruct((n, A_WIDTH + B_WIDTH), BF16),
                   jax.ShapeDtypeStruct((n, B_WIDTH), F32),
                   jax.ShapeDtypeStruct((n_seq, A_HEADS, LANES, 2 * LANES), F32),
                   jax.ShapeDtypeStruct((n_seq, 1, LANES), F32)],
        scratch_shapes=[pltpu.VMEM((A_HEADS, LANES, 2 * LANES), F32), pltpu.VMEM((1, LANES), F32)],
        compiler_params=_params("parallel", "arbitrary"),
        name=name,
    )(p, g, b_if, c0, m0, mh_norm_w, sg_norm_w, sg_w_tril, sg_b_t)


def _pack_state(C, n, m):
    b = C.shape[0]
    cn = jnp.concatenate([C, n[..., None], jnp.zeros(C.shape[:3] + (LANES - 1,), F32)], axis=-1)
    z = jnp.zeros_like(cn)
    cn = cn.reshape(b, A_HEADS // 2, 2, A_DK, 2 * LANES)
    z = z.reshape(cn.shape)
    even = jnp.concatenate([cn[:, :, 0], z[:, :, 0]], axis=-2)
    odd = jnp.concatenate([z[:, :, 1], cn[:, :, 1]], axis=-2)
    c0 = jnp.stack([even, odd], axis=2).reshape(b, A_HEADS, 2 * A_DK, 2 * LANES)
    m0 = jnp.zeros((b, 1, LANES), F32).at[:, 0, A_HEADS:2 * A_HEADS].set(m)
    return c0, m0


def _unpack_state(cfin, mfin):
    b = cfin.shape[0]
    c5 = cfin.reshape(b, A_HEADS // 2, 2, 2, A_DK, 2 * LANES)
    cn = jnp.stack([c5[:, :, 0, 0], c5[:, :, 1, 1]], axis=2).reshape(b, A_HEADS, A_DK, 2 * LANES)
    return cn[..., :A_DV], cn[..., A_DV], mfin[:, 0, A_HEADS:2 * A_HEADS]


def _sb_block(z, vals_bf16, carry, strict_upper, mask):
    sp = _softplus(z)
    if mask is not None:
        sp = jnp.where(mask, sp, 0.0)
    rest = _mm(sp.astype(BF16), strict_upper)
    a = jnp.exp(z - sp - rest - carry)
    if mask is not None:
        a = jnp.where(mask, a, 0.0)
    return _mm(a.astype(BF16), vals_bf16), carry + rest[:, 0:1] + sp[:, 0:1]


def _strict_upper(tk):
    r = lax.broadcasted_iota(jnp.int32, (tk, tk), 0)
    c = lax.broadcasted_iota(jnp.int32, (tk, tk), 1)
    return jnp.where(r > c, 1.0, 0.0).astype(BF16)


def _attn_kernel(bias_ref, q_ref, k_ref, v_ref, o_ref, acc_sc, *, tq):
    tk = ATT_TK
    nd = tq // tk
    h = pl.program_id(1)
    qi = pl.program_id(2)
    bias = bias_ref[h]
    q = q_ref[...]
    su = _strict_upper(tk)
    row = lax.broadcasted_iota(jnp.int32, (tq, tk), 0)
    col = lax.broadcasted_iota(jnp.int32, (tq, tk), 1)
    acc_sc[...] = jnp.zeros_like(acc_sc)

    def block(kb, carry, mask):
        start = pl.multiple_of(kb * tk, tk)
        ks = k_ref[pl.ds(start, tk), :]
        vs = v_ref[pl.ds(start, tk), :]
        z = _nt(q, ks) * SB_SCALE + bias
        out, carry = _sb_block(z, vs, carry, su, mask)
        acc_sc[...] += out
        return carry

    carry = jnp.zeros((tq, 1), F32)
    for sub in reversed(range(nd)):
        carry = block(qi * nd + sub, carry, col + sub * tk < row)
    carry = lax.fori_loop(0, qi * nd, lambda j, cr: block(qi * nd - 1 - j, cr, None), carry)
    o_ref[...] = acc_sc[...].astype(o_ref.dtype)


def _attn(q, k, v, bias, n_seq, tq):
    n = q.shape[0]
    t = n // n_seq
    nq = t // tq
    return pl.pallas_call(
        functools.partial(_attn_kernel, tq=tq),
        grid=(n_seq, C_HEADS, nq),
        in_specs=[pl.BlockSpec(memory_space=pltpu.SMEM),
                  pl.BlockSpec((tq, C_HD), lambda b, h, i: (b * nq + i, h)),
                  pl.BlockSpec((t, C_HD), lambda b, h, i: (b, h)),
                  pl.BlockSpec((t, C_HD), lambda b, h, i: (b, h))],
        out_specs=pl.BlockSpec((tq, C_HD), lambda b, h, i: (b * nq + i, h)),
        out_shape=jax.ShapeDtypeStruct((n, C_WIDTH), BF16),
        scratch_shapes=[pltpu.VMEM((tq, C_HD), F32)],
        compiler_params=_params("parallel", "parallel", "arbitrary"),
        name="sb_attention",
    )(bias, q, k, v)


def _decode_kernel(pt_ref, q_ref, kn_ref, vn_ref, kc_ref, vc_ref, bias_ref, o_ref, acc_sc, carry_sc, *, n_new):
    del pt_ref
    tk = ATT_TK
    R = SAMPLE_ROWS
    j = pl.program_id(1)
    su = _strict_upper(tk)

    def process(new):
        zs, vals = [], []
        for h in range(C_HEADS):
            sl = slice(h * C_HD, (h + 1) * C_HD)
            qh = q_ref[:, sl].astype(BF16)
            if new:
                pad = jnp.zeros((tk - R, C_HD), F32)
                kh = jnp.concatenate([kn_ref[:, sl], pad], axis=0)
                vh = jnp.concatenate([vn_ref[:, sl], pad], axis=0)
            else:
                kh = kc_ref[pl.ds(h, tk, stride=C_HEADS), :]
                vh = vc_ref[pl.ds(h, tk, stride=C_HEADS), :]
            zs.append(_nt(qh, kh.astype(BF16)))
            vals.append(vh.astype(BF16))
        z = jnp.concatenate(zs, axis=0) * SB_SCALE + bias_ref[...]
        mask = None
        if new:
            row = lax.broadcasted_iota(jnp.int32, z.shape, 0)
            col = lax.broadcasted_iota(jnp.int32, z.shape, 1)
            mask = (col < row % R) & (col < n_new)
        sp = _softplus(z)
        if new:
            sp = jnp.where(mask, sp, 0.0)
        hi = sp.astype(BF16)
        lo = (sp - hi.astype(F32)).astype(BF16)
        rest = _mm(hi, su) + _mm(lo, su)
        carry = carry_sc[:, 0:1]
        a = jnp.exp(z - sp - rest - carry)
        if new:
            a = jnp.where(mask, a, 0.0)
        a = a.astype(BF16)
        for h in range(C_HEADS):
            acc_sc[h * R:(h + 1) * R, :] += _mm(a[h * R:(h + 1) * R], vals[h])
        carry_sc[...] = jnp.broadcast_to(carry + rest[:, 0:1] + sp[:, 0:1], carry_sc.shape)

    @pl.when(j == 0)
    def _():
        acc_sc[...] = jnp.zeros_like(acc_sc)
        carry_sc[...] = jnp.zeros_like(carry_sc)
        process(True)

    @pl.when(j > 0)
    def _():
        process(False)

    @pl.when(j == pl.num_programs(1) - 1)
    def _():
        for h in range(C_HEADS):
            o_ref[:, h * C_HD:(h + 1) * C_HD] = acc_sc[h * R:(h + 1) * R, :]


def _decode_attn(q, k_new, v_new, cache_k, cache_v, page_table, bias_rows, layer, n_new):
    n_seq, n_pages = page_table.shape
    n_phys, n_att, page, heads, hd = cache_k.shape
    ck = cache_k.reshape(n_phys, n_att, page * heads, hd)
    cv = cache_v.reshape(n_phys, n_att, page * heads, hd)
    R = SAMPLE_ROWS
    row_map = lambda b, j, pt: (b, 0)
    page_map = lambda b, j, pt: (pt[b, n_pages - jnp.maximum(j, 1)], layer, 0, 0)
    grid_spec = pltpu.PrefetchScalarGridSpec(
        num_scalar_prefetch=1,
        grid=(n_seq, n_pages + 1),
        in_specs=[pl.BlockSpec((R, C_WIDTH), row_map),
                  pl.BlockSpec((R, C_WIDTH), row_map),
                  pl.BlockSpec((R, C_WIDTH), row_map),
                  pl.BlockSpec((None, None, page * heads, hd), page_map),
                  pl.BlockSpec((None, None, page * heads, hd), page_map),
                  pl.BlockSpec((C_HEADS * R, LANES), lambda b, j, pt: (0, 0))],
        out_specs=pl.BlockSpec((R, C_WIDTH), row_map),
        scratch_shapes=[pltpu.VMEM((C_HEADS * R, C_HD), F32), pltpu.VMEM((C_HEADS * R, LANES), F32)],
    )
    return pl.pallas_call(
        functools.partial(_decode_kernel, n_new=n_new),
        grid_spec=grid_spec,
        out_shape=jax.ShapeDtypeStruct((n_seq * R, C_WIDTH), F32),
        compiler_params=_params("parallel", "arbitrary"),
        name="sb_decode",
    )(page_table, q, k_new, v_new, ck, cv, bias_rows)


def _trunk(x, mods, W, n_seq, rows_per_seq, tm, state, valid_len, past):
    n = x.shape[0]
    depth = W['w_up'].shape[0]
    block_rows = min(rows_per_seq, MIX_CHUNK)
    rec, rows, ks, vs = [], [], [], []
    for l in range(depth):
        sh1, sc1, g1, sh2, sc2, g2 = mods[l]
        if l % 2 == 0:
            e = l // 2
            p, gates = _proj(x, W['norm_w'][l, 0], sh1, sc1, W['w_in_ab'][e],
                             [(0, AB_MAIN, F32), (AB_MAIN, AB_MAIN + LANES, F32)], tm, rows_per_seq, "proj_even")
            c0, m0 = _pack_state(*state(e))
            hz, r, cfin, mfin = _mix(p, gates, W['b_if'][e], c0, m0, W['mh_norm_w'][e], W['sg_norm_w'][e],
                                     W['sg_w'][e], W['sg_b'][e], n_seq, block_rows, valid_len, "mix_even")
            rec.append(_unpack_state(cfin, mfin))
            rows.append(r)
            x = _outproj(hz, W['w_out_ab'][e], x, g1, tm, rows_per_seq, "outproj_even")
        else:
            o = l // 2
            if past is None:
                q, k, v, kb, vb = _proj(x, W['norm_w'][l, 0], sh1, sc1, W['w_qkv_c'][o],
                                        [(0, C_WIDTH, BF16), (C_WIDTH, 2 * C_WIDTH, F32),
                                         (2 * C_WIDTH, 3 * C_WIDTH, F32), (C_WIDTH, 2 * C_WIDTH, BF16),
                                         (2 * C_WIDTH, 3 * C_WIDTH, BF16)], tm, rows_per_seq, "proj_odd")
                att = _attn(q, kb, vb, W['b_sb'][o], n_seq, min(256, rows_per_seq))
            else:
                q, k, v = _proj(x, W['norm_w'][l, 0], sh1, sc1, W['w_qkv_c'][o],
                                [(0, C_WIDTH, F32), (C_WIDTH, 2 * C_WIDTH, F32), (2 * C_WIDTH, 3 * C_WIDTH, F32)],
                                tm, rows_per_seq, "proj_odd_s")
                bias_rows = jnp.broadcast_to(jnp.repeat(W['b_sb'][o], SAMPLE_ROWS)[:, None],
                                             (C_HEADS * SAMPLE_ROWS, LANES))
                att = _decode_attn(q, k, v, past[0], past[1], past[2], bias_rows, o, valid_len)
            ks.append(k)
            vs.append(v)
            x = _outproj(att, W['w_out_c'][o], x, g1, tm, rows_per_seq, "outproj_odd")
        x = _mlp(x, W['norm_w'][l, 1], sh2, sc2, g2, W['w_up'][l], W['w_down'][l], W['final_norm_w'],
                 l == depth - 1, tm, 512, rows_per_seq, "mlp")
    return x, rec, rows, ks, vs


def kernel(x_prompt, x_sample, state_mlstm_C, state_mlstm_n, state_mlstm_m, cache_k, cache_v, page_table, c_prompt, c_sample, w_ada, b_ada, norm_w, final_norm_w, w_in_ab, b_if, mh_norm_w, sg_norm_w, sg_w, sg_b, w_out_ab, w_qkv_c, b_sb, w_out_c, w_up, w_down):
    bp, t_p, d = x_prompt.shape
    bs, t_s, _ = x_sample.shape
    depth = w_ada.shape[0]
    n_rec = w_in_ab.shape[0]
    R = SAMPLE_ROWS
    assert t_s <= R and t_p % MIX_CHUNK == 0

    w_in = jnp.concatenate([w_in_ab[:, :, :AB_MAIN], w_in_ab[:, :, AB_MAIN:],
                            jnp.zeros((n_rec, d, LANES - 2 * A_HEADS), F32)], axis=-1).astype(BF16)
    causal = jnp.tril(jnp.ones((MIX_CHUNK, MIX_CHUNK), bool))
    W = dict(
        norm_w=norm_w, final_norm_w=final_norm_w, w_in_ab=w_in,
        b_if=jnp.concatenate([b_if, jnp.zeros((n_rec, LANES - 2 * A_HEADS), F32)], axis=-1).reshape(n_rec, 1, LANES),
        mh_norm_w=mh_norm_w.reshape(n_rec, 1, A_WIDTH), sg_norm_w=sg_norm_w.reshape(n_rec, 1, B_WIDTH),
        sg_w=jnp.where(causal, sg_w, 0.0).astype(BF16),
        sg_b=jnp.concatenate([jnp.swapaxes(sg_b, 1, 2), jnp.zeros((n_rec, MIX_CHUNK, LANES - B_GROUPS), F32)], axis=-1),
        w_out_ab=w_out_ab.astype(BF16), w_qkv_c=w_qkv_c.astype(BF16), b_sb=b_sb, w_out_c=w_out_c.astype(BF16),
        w_up=w_up.astype(BF16), w_down=w_down.astype(BF16))

    n_c = bp + bs
    n_c_pad = -(-n_c // 8) * 8
    c_all = jnp.concatenate([c_prompt, c_sample, jnp.zeros((n_c_pad - n_c, d), F32)], axis=0)
    mod = _ada(c_all, w_ada, b_ada).reshape(depth, n_c_pad, N_MOD, d)
    mods_p = [[mod[l, :bp, i].reshape(bp, 1, d) for i in range(N_MOD)] for l in range(depth)]
    mod_s = jnp.repeat(mod[:, bp:bp + bs], R, axis=1)
    mods_s = [[mod_s[l, :, i].reshape(1, bs * R, d) for i in range(N_MOD)] for l in range(depth)]

    zero_state = (jnp.zeros((bp, A_HEADS, A_DK, A_DV), F32), jnp.zeros((bp, A_HEADS, A_DK), F32),
                  jnp.zeros((bp, A_HEADS), F32))
    yp, rec_p, _, ks_p, vs_p = _trunk(x_prompt.reshape(bp * t_p, d), mods_p, W, bp, t_p, 512,
                                      lambda e: zero_state, MIX_CHUNK, None)

    xs = jnp.pad(x_sample, ((0, 0), (0, R - t_s), (0, 0))).reshape(bs * R, d)
    ys, rec_s, rows_s, ks_s, vs_s = _trunk(
        xs, mods_s, W, bs, R, bs * R,
        lambda e: (state_mlstm_C[e], state_mlstm_n[e], state_mlstm_m[e]), t_s, (cache_k, cache_v, page_table))

    def unpad(a, tail):
        return a.reshape((bs, R) + tail)[:, :t_s]

    return (yp.reshape(bp, t_p, d), unpad(ys, (d,)),
            jnp.stack([r[0] for r in rec_p]), jnp.stack([r[1] for r in rec_p]), jnp.stack([r[2] for r in rec_p]),
            jnp.stack([r[0] for r in rec_s]), jnp.stack([r[1] for r in rec_s]), jnp.stack([r[2] for r in rec_s]),
            jnp.stack([unpad(r, (B_WIDTH,)) for r in rows_s]),
            jnp.stack([k.reshape(bp, t_p, C_HEADS, C_HD) for k in ks_p]),
            jnp.stack([v.reshape(bp, t_p, C_HEADS, C_HD) for v in vs_p]),
            jnp.stack([unpad(k, (C_HEADS, C_HD)) for k in ks_s]),
            jnp.stack([unpad(v, (C_HEADS, C_HD)) for v in vs_s]))
```

```python
import functools

import jax
import jax.numpy as jnp
from jax import lax
from jax.experimental import pallas as pl
from jax.experimental.pallas import tpu as pltpu

F32 = jnp.float32
BF16 = jnp.bfloat16

D_MODEL = 1024
A_HEADS = 4
A_DV = 128
A_DK = 64
A_WIDTH = A_HEADS * A_DV
A_QK_W = A_HEADS * A_DK
B_GROUPS = 4
B_WIDTH = 512
B_GROUP_DIM = 128
C_HEADS = 8
C_HD = 128
C_WIDTH = C_HEADS * C_HD
SB_SCALE = C_HD ** -0.5
D_FF = 4 * D_MODEL
N_MOD = 6
EPS = 1e-6
AB_MAIN = 2 * A_QK_W + 2 * A_WIDTH + 2 * B_WIDTH
MIX_CHUNK = 128
ATT_TK = 128
PROMPT_TK = 256
PROMPT_TQ = 512
DECODE_PAGES = 4
MLP_TM = 1024
MLP_TF = 512
LOG2E = 1.4426950408889634
EXP2_MAX = 126.0
SAMPLE_ROWS = 8
NEG = -1e30
LANES = 128
VMEM_LIMIT_V7X = 56 * 1024 * 1024


def _mm(a, b):
    return jnp.dot(a, b, preferred_element_type=F32)


def _nt(a, b):
    return lax.dot_general(a, b, (((1,), (1,)), ((), ())), preferred_element_type=F32)


def _tn(a, b):
    return lax.dot_general(a, b, (((0,), (0,)), ((), ())), preferred_element_type=F32)


def _softplus(z):
    return jnp.maximum(z, 0.0) + jnp.log1p(jnp.exp(-jnp.abs(z)))


def _rms(x, w):
    return x * lax.rsqrt(jnp.mean(x * x, axis=-1, keepdims=True) + EPS) * w


def _rms_mod(x, w, shift, scale):
    return _rms(x, w) * (1.0 + scale) + shift


def _split3(x):
    p1 = x.astype(BF16)
    r1 = x - p1.astype(F32)
    p2 = r1.astype(BF16)
    p3 = (r1 - p2.astype(F32)).astype(BF16)
    return p1, p2, p3


def _params(*sem):
    return pltpu.CompilerParams(dimension_semantics=sem, vmem_limit_bytes=VMEM_LIMIT_V7X)


def _mod_spec(arr, tm, rows_per_group):
    g, r, d = arr.shape
    assert r == 1 or r == tm
    bpg = rows_per_group // tm if r == 1 else 1
    return pl.BlockSpec((None, r, d), lambda i, *_: (i // bpg, 0, 0))


def _ada_kernel(c_ref, w_ref, b_ref, o_ref):
    c = c_ref[...]
    s = (c * jax.nn.sigmoid(c)).astype(BF16)
    o_ref[...] = _mm(s, w_ref[...].astype(BF16)) + b_ref[...]


def _ada(c_all, w_ada, b_ada):
    depth, d, n = w_ada.shape
    rows = c_all.shape[0]
    tn = 1536
    return pl.pallas_call(
        _ada_kernel,
        grid=(depth, n // tn),
        in_specs=[pl.BlockSpec((rows, d), lambda l, j: (0, 0)),
                  pl.BlockSpec((None, d, tn), lambda l, j: (l, 0, j)),
                  pl.BlockSpec((None, 1, tn), lambda l, j: (l, 0, j))],
        out_specs=pl.BlockSpec((None, rows, tn), lambda l, j: (l, 0, j)),
        out_shape=jax.ShapeDtypeStruct((depth, rows, n), F32),
        compiler_params=_params("parallel", "parallel"),
        name="ada_mod",
    )(c_all, w_ada, b_ada.reshape(depth, 1, n))


def _proj_kernel(x_ref, nw_ref, sh_ref, sc_ref, w_ref, *o_refs, outs):
    h = _rms_mod(x_ref[...], nw_ref[...], sh_ref[...], sc_ref[...]).astype(BF16)
    products = {}
    for o_ref, (a, b, _, mult, per_head) in zip(o_refs, outs):
        if (a, b) not in products:
            products[(a, b)] = _mm(h, w_ref[:, a:b])
        y = products[(a, b)]
        if mult is not None:
            y = y * mult
        if per_head:
            o_ref[...] = y.reshape(o_ref.shape).astype(o_ref.dtype)
        else:
            o_ref[...] = y.astype(o_ref.dtype)


def _proj(x, nw, shift, scale, w, outs, tm, rows_per_group, name):
    n, d = x.shape
    out_specs, out_shape = [], []
    for a, b, dt, _, per_head in outs:
        if per_head:
            out_specs.append(pl.BlockSpec((tm, (b - a) // LANES, LANES), lambda i: (i, 0, 0)))
            out_shape.append(jax.ShapeDtypeStruct((n, (b - a) // LANES, LANES), dt))
        else:
            out_specs.append(pl.BlockSpec((tm, b - a), lambda i: (i, 0)))
            out_shape.append(jax.ShapeDtypeStruct((n, b - a), dt))
    return pl.pallas_call(
        functools.partial(_proj_kernel, outs=tuple(outs)),
        grid=(n // tm,),
        in_specs=[pl.BlockSpec((tm, d), lambda i: (i, 0)),
                  pl.BlockSpec((1, d), lambda i: (0, 0)),
                  _mod_spec(shift, tm, rows_per_group),
                  _mod_spec(scale, tm, rows_per_group),
                  pl.BlockSpec(w.shape, lambda i: (0, 0))],
        out_specs=out_specs,
        out_shape=out_shape,
        compiler_params=_params("parallel"),
        name=name,
    )(x, nw.reshape(1, d), shift, scale, w)


def _tail_kernel(a_ref, wo_ref, x_ref, g1_ref, nw_ref, sh_ref, sc_ref, g2_ref, wu_ref, wd_ref, fnw_ref,
                 o_ref, h_sc, acc_sc, *, final):
    f = pl.program_id(1)

    @pl.when(f == 0)
    def _():
        x1 = x_ref[...] + g1_ref[...] * _mm(a_ref[...].astype(BF16), wo_ref[...])
        o_ref[...] = x1
        h_sc[...] = _rms_mod(x1, nw_ref[...], sh_ref[...], sc_ref[...]).astype(BF16)
        acc_sc[...] = jnp.zeros_like(acc_sc)

    up = _mm(h_sc[...], wu_ref[...])
    act = jnp.square(jnp.maximum(up, 0.0)).astype(BF16)
    acc_sc[...] += _mm(act, wd_ref[...])

    @pl.when(f == pl.num_programs(1) - 1)
    def _():
        y = o_ref[...] + g2_ref[...] * acc_sc[...]
        if final:
            y = _rms(y, fnw_ref[...])
        o_ref[...] = y


def _tail(a, w_out, x, gate1, nw, shift, scale, gate2, w_up, w_down, fnw, final, tm, tf, rows_per_group, name):
    n, d = x.shape
    dff = w_up.shape[1]
    rows = lambda i, f: (i, 0)
    const = lambda i, f: (0, 0)
    return pl.pallas_call(
        functools.partial(_tail_kernel, final=final),
        grid=(n // tm, dff // tf),
        in_specs=[pl.BlockSpec((tm, a.shape[1]), rows),
                  pl.BlockSpec(w_out.shape, const),
                  pl.BlockSpec((tm, d), rows),
                  _mod_spec(gate1, tm, rows_per_group),
                  pl.BlockSpec((1, d), const),
                  _mod_spec(shift, tm, rows_per_group),
                  _mod_spec(scale, tm, rows_per_group),
                  _mod_spec(gate2, tm, rows_per_group),
                  pl.BlockSpec((d, tf), lambda i, f: (0, f)),
                  pl.BlockSpec((tf, d), lambda i, f: (f, 0)),
                  pl.BlockSpec((1, d), const)],
        out_specs=pl.BlockSpec((tm, d), rows),
        out_shape=jax.ShapeDtypeStruct((n, d), F32),
        scratch_shapes=[pltpu.VMEM((tm, d), BF16), pltpu.VMEM((tm, d), F32)],
        compiler_params=_params("parallel", "arbitrary"),
        name=name,
    )(a, w_out, x, gate1, nw.reshape(1, d), shift, scale, gate2, w_up, w_down, fnw.reshape(1, d))


def _mix_kernel(p_ref, g_ref, bif_ref, c0_ref, m0_ref, mhw_ref, sgnw_ref, sgw_ref, sgb_ref,
                hz_ref, rows_ref, cfin_ref, mfin_ref, c_sc, m_sc, *, block_rows, valid_len):
    L = MIX_CHUNK
    c = pl.program_id(1)

    @pl.when(c == 0)
    def _():
        c_sc[...] = c0_ref[...]
        m_sc[...] = m0_ref[...]

    p = p_ref[...]
    g = g_ref[...]
    if block_rows < L:
        p = jnp.concatenate([p, jnp.zeros((L - block_rows, p.shape[1]), F32)], axis=0)
        g = jnp.concatenate([g, jnp.zeros((L - block_rows, g.shape[1]), F32)], axis=0)

    rowi = lax.broadcasted_iota(jnp.int32, (L, LANES), 0)
    lanei = lax.broadcasted_iota(jnp.int32, (L, LANES), 1)
    is_i = lanei < A_HEADS
    is_f = (lanei >= A_HEADS) & (lanei < 2 * A_HEADS)
    gb = g + bif_ref[...]
    x8 = jnp.where(is_i, gb, jnp.where(is_f, -_softplus(-gb), 0.0))
    if valid_len < L:
        x8 = jnp.where(rowi < valid_len, x8, jnp.where(is_i, NEG, 0.0))

    sq_r = lax.broadcasted_iota(jnp.int32, (L, L), 0)
    sq_c = lax.broadcasted_iota(jnp.int32, (L, L), 1)
    causal = sq_c <= sq_r
    tril = jnp.where(causal, 1.0, 0.0).astype(BF16)
    eye = jnp.where(sq_c == sq_r, 1.0, 0.0).astype(BF16)

    parts = _split3(x8)
    bc = _mm(tril, parts[0]) + _mm(tril, parts[1]) + _mm(tril, parts[2])
    parts_t = [_nt(eye, q).astype(BF16) for q in parts]
    x8_t = parts_t[0].astype(F32) + parts_t[1].astype(F32) + parts_t[2].astype(F32)
    bc_t = _nt(parts_t[0], tril) + _nt(parts_t[1], tril) + _nt(parts_t[2], tril)

    lane_l = lax.broadcasted_iota(jnp.int32, (L, LANES), 1)
    e1 = jnp.where(lane_l == 0, 1.0, 0.0)
    m_row = m_sc[...]
    m_row_new = m_row
    has = []
    for h in range(A_HEADS):
        hp, hl = h // 2, h % 2
        headmask = (lane_l >= A_DK * hl) & (lane_l < A_DK * (hl + 1))
        q2 = p[:, hp * LANES:(hp + 1) * LANES]
        k2 = p[:, A_QK_W + hp * LANES:A_QK_W + (hp + 1) * LANES]
        v = p[:, 2 * A_QK_W + h * A_DV:2 * A_QK_W + (h + 1) * A_DV]
        qh = jnp.where(headmask, q2, 0.0).astype(BF16)
        kh = jnp.where(headmask, k2 * (A_DK ** -0.5), 0.0)
        ig_row = x8_t[h:h + 1, :]
        b_row = bc_t[A_HEADS + h:A_HEADS + h + 1, :]
        ig_col = x8[:, h:h + 1]
        b_col = bc[:, A_HEADS + h:A_HEADS + h + 1]
        m_prev = m_row[:, A_HEADS + h:A_HEADS + h + 1]

        d = jnp.where(causal, b_col - b_row + ig_row, NEG)
        inter = b_col + m_prev
        m_t = jnp.maximum(inter, jnp.max(d, axis=1, keepdims=True))
        w_st = jnp.exp(inter - m_t)
        pmat = _nt(qh, kh.astype(BF16)) * jnp.exp(d - m_t)
        v_ext = jnp.concatenate([v, e1], axis=1).astype(BF16)
        c_old = c_sc[h]
        num_ext = _mm(pmat.astype(BF16), v_ext) + w_st * _mm(qh, c_old.astype(BF16))
        den = num_ext[:, A_DV:A_DV + 1]
        hh = num_ext[:, :A_DV] / jnp.maximum(jnp.abs(den), jnp.exp(-m_t))

        m_new = m_t[L - 1:L, :]
        b_end = b_col[L - 1:L, :]
        w_k = jnp.exp(b_end - b_col + ig_col - m_new)
        decay = jnp.exp(b_end + m_prev - m_new)
        c_sc[h] = decay * c_old + _tn((kh * w_k).astype(BF16), v_ext)
        m_row_new = jnp.where(lax.broadcasted_iota(jnp.int32, m_row.shape, 1) == A_HEADS + h, m_new, m_row_new)

        o = p[:, 2 * A_QK_W + A_WIDTH + h * A_DV:2 * A_QK_W + A_WIDTH + (h + 1) * A_DV]
        has.append(_rms(hh, mhw_ref[:, h * A_DV:(h + 1) * A_DV]) * jax.nn.sigmoid(o))
    m_sc[...] = m_row_new

    u0 = 2 * A_QK_W + 2 * A_WIDTH
    zbs, vsn = [], []
    for gi in range(B_GROUPS):
        u = jax.nn.gelu(p[:, u0 + gi * B_GROUP_DIM:u0 + (gi + 1) * B_GROUP_DIM])
        vs = jax.nn.gelu(p[:, u0 + B_WIDTH + gi * B_GROUP_DIM:u0 + B_WIDTH + (gi + 1) * B_GROUP_DIM])
        vs = _rms(vs, sgnw_ref[:, gi * B_GROUP_DIM:(gi + 1) * B_GROUP_DIM])
        s = _mm(sgw_ref[gi], vs.astype(BF16)) + sgb_ref[:, gi:gi + 1]
        zbs.append(u * s)
        vsn.append(vs)

    hz = jnp.concatenate(has + zbs, axis=1)
    hz_ref[...] = hz[:block_rows].astype(hz_ref.dtype)
    rows_ref[...] = jnp.concatenate(vsn, axis=1)[:block_rows]

    @pl.when(c == pl.num_programs(1) - 1)
    def _():
        cfin_ref[...] = c_sc[...]
        mfin_ref[...] = m_sc[...]


def _mix(p, g, b_if, c0, m0, mh_norm_w, sg_norm_w, sg_w_tril, sg_b_t, n_seq, block_rows, valid_len, name):
    n = p.shape[0]
    nc = n // (n_seq * block_rows)
    row_map = lambda b, c: (b * nc + c, 0)
    const2 = lambda b, c: (0, 0)
    return pl.pallas_call(
        functools.partial(_mix_kernel, block_rows=block_rows, valid_len=valid_len),
        grid=(n_seq, nc),
        in_specs=[pl.BlockSpec((block_rows, AB_MAIN), row_map),
                  pl.BlockSpec((block_rows, LANES), row_map),
                  pl.BlockSpec((1, LANES), const2),
                  pl.BlockSpec((None, A_HEADS, LANES, 2 * LANES), lambda b, c: (b, 0, 0, 0)),
                  pl.BlockSpec((None, 1, LANES), lambda b, c: (b, 0, 0)),
                  pl.BlockSpec((1, A_WIDTH), const2),
                  pl.BlockSpec((1, B_WIDTH), const2),
                  pl.BlockSpec((B_GROUPS, MIX_CHUNK, MIX_CHUNK), lambda b, c: (0, 0, 0)),
                  pl.BlockSpec((MIX_CHUNK, LANES), const2)],
        out_specs=[pl.BlockSpec((block_rows, A_WIDTH + B_WIDTH), row_map),
                   pl.BlockSpec((block_rows, B_WIDTH), row_map),
                   pl.BlockSpec((None, A_HEADS, LANES, 2 * LANES), lambda b, c: (b, 0, 0, 0)),
                   pl.BlockSpec((None, 1, LANES), lambda b, c: (b, 0, 0))],
        out_shape=[jax.ShapeDtypeStruct((n, A_WIDTH + B_WIDTH), BF16),
                   jax.ShapeDtypeStruct((n, B_WIDTH), F32),
                   jax.ShapeDtypeStruct((n_seq, A_HEADS, LANES, 2 * LANES), F32),
                   jax.ShapeDtypeStruct((n_seq, 1, LANES), F32)],
        scratch_shapes=[pltpu.VMEM((A_HEADS, LANES, 2 * LANES), F32), pltpu.VMEM((1, LANES), F32)],
        compiler_params=_params("parallel", "arbitrary"),
        name=name,
    )(p, g, b_if, c0, m0, mh_norm_w, sg_norm_w, sg_w_tril, sg_b_t)


def _pack_state(C, n, m):
    b = C.shape[0]
    cn = jnp.concatenate([C, n[..., None], jnp.zeros(C.shape[:3] + (LANES - 1,), F32)], axis=-1)
    z = jnp.zeros_like(cn)
    cn = cn.reshape(b, A_HEADS // 2, 2, A_DK, 2 * LANES)
    z = z.reshape(cn.shape)
    even = jnp.concatenate([cn[:, :, 0], z[:, :, 0]], axis=-2)
    odd = jnp.concatenate([z[:, :, 1], cn[:, :, 1]], axis=-2)
    c0 = jnp.stack([even, odd], axis=2).reshape(b, A_HEADS, 2 * A_DK, 2 * LANES)
    m0 = jnp.zeros((b, 1, LANES), F32).at[:, 0, A_HEADS:2 * A_HEADS].set(m)
    return c0, m0


def _unpack_state(cfin, mfin):
    b = cfin.shape[0]
    c5 = cfin.reshape(b, A_HEADS // 2, 2, 2, A_DK, 2 * LANES)
    cn = jnp.stack([c5[:, :, 0, 0], c5[:, :, 1, 1]], axis=2).reshape(b, A_HEADS, A_DK, 2 * LANES)
    return cn[..., :A_DV], cn[..., A_DV], mfin[:, 0, A_HEADS:2 * A_HEADS]


def _strict_upper(tk):
    r = lax.broadcasted_iota(jnp.int32, (tk, tk), 0)
    c = lax.broadcasted_iota(jnp.int32, (tk, tk), 1)
    return jnp.where(r > c, 1.0, 0.0).astype(BF16)


def _attn_kernel(bias_ref, q_ref, k_ref, v_ref, o_ref, acc_sc, *, tq):
    tk = PROMPT_TK
    nd = tq // tk
    h = pl.program_id(1)
    qi = pl.program_id(2)
    bias2 = bias_ref[h] * LOG2E
    nsu = -_strict_upper(tk)

    def block(q_rows, kb, ncarry, mask, nsub=1):
        start = pl.multiple_of((kb - (nsub - 1)) * tk, tk)
        ks = k_ref[pl.ds(start, nsub * tk), :]
        vs = v_ref[pl.ds(start, nsub * tk), :]
        z2 = _nt(q_rows, ks) + bias2
        e = jnp.exp2(jnp.minimum(z2, EXP2_MAX))
        sp = jnp.log(1.0 + e)
        if mask is not None:
            sp = jnp.where(mask, sp, 0.0)
        parts = [None] * nsub
        for i in reversed(range(nsub)):
            sp_i = sp[:, i * tk:(i + 1) * tk]
            nrest = _mm(sp_i.astype(BF16), nsu)
            parts[i] = e[:, i * tk:(i + 1) * tk] * jnp.exp(nrest + ncarry - sp_i)
            ncarry = ncarry + nrest[:, 0:1] - sp_i[:, 0:1]
        a = parts[0] if nsub == 1 else jnp.concatenate(parts, axis=1)
        if mask is not None:
            a = jnp.where(mask, a, 0.0)
        return _mm(a.astype(BF16), vs), ncarry

    row = lax.broadcasted_iota(jnp.int32, (tk, tk), 0)
    col = lax.broadcasted_iota(jnp.int32, (tk, tk), 1)
    ncarries = []
    for rb in range(nd):
        q_rows = q_ref[rb * tk:(rb + 1) * tk, :]
        ncarry = jnp.zeros((tk, 1), F32)
        acc = jnp.zeros((tk, C_HD), F32)
        for sub in reversed(range(rb + 1)):
            out, ncarry = block(q_rows, qi * nd + sub, ncarry, col < row if sub == rb else None)
            acc = acc + out
        acc_sc[rb * tk:(rb + 1) * tk, :] = acc
        ncarries.append(ncarry)

    def below(j, ncarry):
        out, ncarry = block(q_ref[...], (qi - j) * nd - 1, ncarry, None, nsub=nd)
        acc_sc[...] += out
        return ncarry

    lax.fori_loop(0, qi, below, jnp.concatenate(ncarries, axis=0))
    o_ref[...] = acc_sc[...].astype(o_ref.dtype)


def _attn(q, k, v, bias, n_seq, tq):
    n = q.shape[0]
    t = n // n_seq
    nq = t // tq
    return pl.pallas_call(
        functools.partial(_attn_kernel, tq=tq),
        grid=(n_seq, C_HEADS, nq),
        in_specs=[pl.BlockSpec(memory_space=pltpu.SMEM),
                  pl.BlockSpec((tq, C_HD), lambda b, h, i: (b * nq + i, h)),
                  pl.BlockSpec((t, C_HD), lambda b, h, i: (b, h)),
                  pl.BlockSpec((t, C_HD), lambda b, h, i: (b, h))],
        out_specs=pl.BlockSpec((tq, C_HD), lambda b, h, i: (b * nq + i, h)),
        out_shape=jax.ShapeDtypeStruct((n, C_WIDTH), BF16),
        scratch_shapes=[pltpu.VMEM((tq, C_HD), F32)],
        compiler_params=_params("parallel", "parallel", "arbitrary"),
        name="sb_attention",
    )(bias, q, k, v)


def _decode_kernel(pt_ref, q_ref, kn_ref, vn_ref, *refs, n_new, n_pg):
    del pt_ref
    kc_refs, vc_refs = refs[:n_pg], refs[n_pg:2 * n_pg]
    bias_ref, o_ref, acc_sc, carry_sc = refs[2 * n_pg:]
    tk = ATT_TK
    R = SAMPLE_ROWS
    HR = C_HEADS * R
    j = pl.program_id(1)
    su = _strict_upper(tk)

    def weights(z, mask, carry_in):
        sp = _softplus(z)
        if mask is not None:
            sp = jnp.where(mask, sp, 0.0)
        hi = sp.astype(BF16)
        lo = (sp - hi.astype(F32)).astype(BF16)
        rest = _mm(hi, su) + _mm(lo, su)
        tot = rest[:, 0:1] + sp[:, 0:1]
        carries = [carry_in]
        for p in range(z.shape[0] // HR):
            carries.append(carries[-1] + tot[p * HR:(p + 1) * HR])
        a = jnp.exp(z - sp - rest - jnp.concatenate(carries[:-1], axis=0))
        if mask is not None:
            a = jnp.where(mask, a, 0.0)
        return a, carries[-1]

    @pl.when(j == 0)
    def _():
        pad = jnp.zeros((tk - R, C_HD), F32)
        zs, vals = [], []
        for h in range(C_HEADS):
            sl = slice(h * C_HD, (h + 1) * C_HD)
            kh = jnp.concatenate([kn_ref[:, sl], pad], axis=0).astype(BF16)
            vals.append(jnp.concatenate([vn_ref[:, sl], pad], axis=0).astype(BF16))
            zs.append(_nt(q_ref[:, sl].astype(BF16), kh))
        z = jnp.concatenate(zs, axis=0) * SB_SCALE + bias_ref[...]
        row = lax.broadcasted_iota(jnp.int32, z.shape, 0)
        col = lax.broadcasted_iota(jnp.int32, z.shape, 1)
        a, carry = weights(z, (col < row % R) & (col < n_new), jnp.zeros((HR, 1), F32))
        for h in range(C_HEADS):
            acc_sc[h * R:(h + 1) * R, :] = _mm(a[h * R:(h + 1) * R].astype(BF16), vals[h])
        carry_sc[...] = jnp.broadcast_to(carry, carry_sc.shape)

    zs = [[None] * C_HEADS for _ in range(n_pg)]
    vals = []
    for h in range(C_HEADS):
        sl = slice(h * C_HD, (h + 1) * C_HD)
        kh = jnp.concatenate([r[pl.ds(h, tk, stride=C_HEADS), :].astype(BF16) for r in kc_refs], axis=0)
        vals.append(jnp.concatenate([r[pl.ds(h, tk, stride=C_HEADS), :].astype(BF16) for r in vc_refs], axis=0))
        zh = _nt(q_ref[:, sl].astype(BF16), kh)
        for p in range(n_pg):
            zs[p][h] = zh[:, p * tk:(p + 1) * tk]
    z = jnp.concatenate([zs[p][h] for p in range(n_pg) for h in range(C_HEADS)], axis=0)
    z = z * SB_SCALE + jnp.concatenate([bias_ref[...]] * n_pg, axis=0)
    a, carry = weights(z, None, carry_sc[:, 0:1])
    for h in range(C_HEADS):
        ah = jnp.concatenate([a[(p * C_HEADS + h) * R:(p * C_HEADS + h + 1) * R] for p in range(n_pg)], axis=1)
        acc_sc[h * R:(h + 1) * R, :] += _mm(ah.astype(BF16), vals[h])
    carry_sc[...] = jnp.broadcast_to(carry, carry_sc.shape)

    @pl.when(j == pl.num_programs(1) - 1)
    def _():
        for h in range(C_HEADS):
            o_ref[:, h * C_HD:(h + 1) * C_HD] = acc_sc[h * R:(h + 1) * R, :]


def _decode_attn(q, k_new, v_new, cache_k, cache_v, page_table, bias_rows, layer, n_new):
    n_seq, n_pages = page_table.shape
    n_phys, n_att, page, heads, hd = cache_k.shape
    n_pg = DECODE_PAGES
    assert n_pages % n_pg == 0 and page == ATT_TK
    ck = cache_k.reshape(n_phys, n_att, page * heads, hd)
    cv = cache_v.reshape(n_phys, n_att, page * heads, hd)
    R = SAMPLE_ROWS
    row_map = lambda b, j, pt: (b, 0)

    def page_spec(p):
        return pl.BlockSpec((None, None, page * heads, hd),
                            lambda b, j, pt: (pt[b, n_pages - 1 - (j * n_pg + p)], layer, 0, 0))

    grid_spec = pltpu.PrefetchScalarGridSpec(
        num_scalar_prefetch=1,
        grid=(n_seq, n_pages // n_pg),
        in_specs=[pl.BlockSpec((R, C_WIDTH), row_map)] * 3
                 + [page_spec(p) for p in range(n_pg)] * 2
                 + [pl.BlockSpec((C_HEADS * R, LANES), lambda b, j, pt: (0, 0))],
        out_specs=pl.BlockSpec((R, C_WIDTH), row_map),
        scratch_shapes=[pltpu.VMEM((C_HEADS * R, C_HD), F32), pltpu.VMEM((C_HEADS * R, LANES), F32)],
    )
    return pl.pallas_call(
        functools.partial(_decode_kernel, n_new=n_new, n_pg=n_pg),
        grid_spec=grid_spec,
        out_shape=jax.ShapeDtypeStruct((n_seq * R, C_WIDTH), F32),
        compiler_params=_params("parallel", "arbitrary"),
        name="sb_decode",
    )(page_table, q, k_new, v_new, *([ck] * n_pg), *([cv] * n_pg), bias_rows)


def _trunk(x, mods, W, n_seq, rows_per_seq, tm, state, valid_len, past):
    n = x.shape[0]
    depth = W['w_up'].shape[0]
    block_rows = min(rows_per_seq, MIX_CHUNK)
    rec, rows, ks, vs = [], [], [], []
    for l in range(depth):
        sh1, sc1, g1, sh2, sc2, g2 = mods[l]
        if l % 2 == 0:
            e = l // 2
            p, gates = _proj(x, W['norm_w'][l, 0], sh1, sc1, W['w_in_ab'][e],
                             [(0, AB_MAIN, F32, None, False), (AB_MAIN, AB_MAIN + LANES, F32, None, False)],
                             tm, rows_per_seq, "proj_even")
            c0, m0 = _pack_state(*state(e))
            mixed, r, cfin, mfin = _mix(p, gates, W['b_if'][e], c0, m0, W['mh_norm_w'][e], W['sg_norm_w'][e],
                                        W['sg_w'][e], W['sg_b'][e], n_seq, block_rows, valid_len, "mix_even")
            rec.append(_unpack_state(cfin, mfin))
            rows.append(r)
            w_out = W['w_out_ab'][e]
        else:
            o = l // 2
            kcols, vcols = (C_WIDTH, 2 * C_WIDTH), (2 * C_WIDTH, 3 * C_WIDTH)
            if past is None:
                q, k, v, kb, vb = _proj(x, W['norm_w'][l, 0], sh1, sc1, W['w_qkv_c'][o],
                                        [(0, C_WIDTH, BF16, SB_SCALE * LOG2E, False),
                                         kcols + (F32, None, True), vcols + (F32, None, True),
                                         kcols + (BF16, None, False), vcols + (BF16, None, False)],
                                        tm, rows_per_seq, "proj_odd")
                mixed = _attn(q, kb, vb, W['b_sb'][o], n_seq, min(PROMPT_TQ, rows_per_seq))
            else:
                q, k, v = _proj(x, W['norm_w'][l, 0], sh1, sc1, W['w_qkv_c'][o],
                                [(0, C_WIDTH, F32, None, False), kcols + (F32, None, False),
                                 vcols + (F32, None, False)], tm, rows_per_seq, "proj_odd_s")
                bias_rows = jnp.broadcast_to(jnp.repeat(W['b_sb'][o], SAMPLE_ROWS)[:, None],
                                             (C_HEADS * SAMPLE_ROWS, LANES))
                mixed = _decode_attn(q, k, v, past[0], past[1], past[2], bias_rows, o, valid_len)
            ks.append(k)
            vs.append(v)
            w_out = W['w_out_c'][o]
        x = _tail(mixed, w_out, x, g1, W['norm_w'][l, 1], sh2, sc2, g2, W['w_up'][l], W['w_down'][l],
                  W['final_norm_w'], l == depth - 1, min(MLP_TM, n), MLP_TF, rows_per_seq, "tail")
    return x, rec, rows, ks, vs


def kernel(x_prompt, x_sample, state_mlstm_C, state_mlstm_n, state_mlstm_m, cache_k, cache_v, page_table, c_prompt, c_sample, w_ada, b_ada, norm_w, final_norm_w, w_in_ab, b_if, mh_norm_w, sg_norm_w, sg_w, sg_b, w_out_ab, w_qkv_c, b_sb, w_out_c, w_up, w_down):
    bp, t_p, d = x_prompt.shape
    bs, t_s, _ = x_sample.shape
    depth = w_ada.shape[0]
    n_rec = w_in_ab.shape[0]
    R = SAMPLE_ROWS
    assert t_s <= R and t_p % MIX_CHUNK == 0

    w_in = jnp.concatenate([w_in_ab[:, :, :AB_MAIN], w_in_ab[:, :, AB_MAIN:],
                            jnp.zeros((n_rec, d, LANES - 2 * A_HEADS), F32)], axis=-1).astype(BF16)
    causal = jnp.tril(jnp.ones((MIX_CHUNK, MIX_CHUNK), bool))
    W = dict(
        norm_w=norm_w, final_norm_w=final_norm_w, w_in_ab=w_in,
        b_if=jnp.concatenate([b_if, jnp.zeros((n_rec, LANES - 2 * A_HEADS), F32)], axis=-1).reshape(n_rec, 1, LANES),
        mh_norm_w=mh_norm_w.reshape(n_rec, 1, A_WIDTH), sg_norm_w=sg_norm_w.reshape(n_rec, 1, B_WIDTH),
        sg_w=jnp.where(causal, sg_w, 0.0).astype(BF16),
        sg_b=jnp.concatenate([jnp.swapaxes(sg_b, 1, 2), jnp.zeros((n_rec, MIX_CHUNK, LANES - B_GROUPS), F32)], axis=-1),
        w_out_ab=w_out_ab.astype(BF16), w_qkv_c=w_qkv_c.astype(BF16), b_sb=b_sb, w_out_c=w_out_c.astype(BF16),
        w_up=w_up.astype(BF16), w_down=w_down.astype(BF16))

    n_c = bp + bs
    n_c_pad = -(-n_c // 8) * 8
    c_all = jnp.concatenate([c_prompt, c_sample, jnp.zeros((n_c_pad - n_c, d), F32)], axis=0)
    mod = _ada(c_all, w_ada, b_ada).reshape(depth, n_c_pad, N_MOD, d)
    mods_p = [[mod[l, :bp, i].reshape(bp, 1, d) for i in range(N_MOD)] for l in range(depth)]
    mod_s = jnp.repeat(mod[:, bp:bp + bs], R, axis=1)
    mods_s = [[mod_s[l, :, i].reshape(1, bs * R, d) for i in range(N_MOD)] for l in range(depth)]

    zero_state = (jnp.zeros((bp, A_HEADS, A_DK, A_DV), F32), jnp.zeros((bp, A_HEADS, A_DK), F32),
                  jnp.zeros((bp, A_HEADS), F32))
    yp, rec_p, _, ks_p, vs_p = _trunk(x_prompt.reshape(bp * t_p, d), mods_p, W, bp, t_p, 512,
                                      lambda e: zero_state, MIX_CHUNK, None)

    xs = jnp.pad(x_sample, ((0, 0), (0, R - t_s), (0, 0))).reshape(bs * R, d)
    ys, rec_s, rows_s, ks_s, vs_s = _trunk(
        xs, mods_s, W, bs, R, bs * R,
        lambda e: (state_mlstm_C[e], state_mlstm_n[e], state_mlstm_m[e]), t_s, (cache_k, cache_v, page_table))

    def unpad(a, tail):
        return a.reshape((bs, R) + tail)[:, :t_s]

    return (yp.reshape(bp, t_p, d), unpad(ys, (d,)),
            jnp.stack([r[0] for r in rec_p]), jnp.stack([r[1] for r in rec_p]), jnp.stack([r[2] for r in rec_p]),
            jnp.stack([r[0] for r in rec_s]), jnp.stack([r[1] for r in rec_s]), jnp.stack([r[2] for r in rec_s]),
            jnp.stack([unpad(r, (B_WIDTH,)) for r in rows_s]),
            jnp.stack([k.reshape(bp, t_p, C_HEADS, C_HD) for k in ks_p]),
            jnp.stack([v.reshape(bp, t_p, C_HEADS, C_HD) for v in vs_p]),
            jnp.stack([unpad(k, (C_HEADS, C_HD)) for k in ks_s]),
            jnp.stack([unpad(v, (C_HEADS, C_HD)) for v in vs_s]))
```

```python
import functools

import jax
import jax.numpy as jnp
from jax import lax
from jax.experimental import pallas as pl
from jax.experimental.pallas import tpu as pltpu

F32 = jnp.float32
BF16 = jnp.bfloat16

D_MODEL = 1024
A_HEADS = 4
A_DV = 128
A_DK = 64
A_WIDTH = A_HEADS * A_DV
A_QK_W = A_HEADS * A_DK
B_GROUPS = 4
B_WIDTH = 512
B_GROUP_DIM = 128
C_HEADS = 8
C_HD = 128
C_WIDTH = C_HEADS * C_HD
SB_SCALE = C_HD ** -0.5
D_FF = 4 * D_MODEL
N_MOD = 6
EPS = 1e-6
AB_MAIN = 2 * A_QK_W + 2 * A_WIDTH + 2 * B_WIDTH
MIX_CHUNK = 128
MIX_CHUNKS_PER_STEP = 4
ATT_TK = 128
PROMPT_TK = 256
PROMPT_TQ = 512
PROMPT_BLOCKS_PER_TRIP = 2
DECODE_PAGES = 8
MLP_TM = 1024
MLP_TF = 512
LOG2E = 1.4426950408889634
EXP2_MAX = 126.0
SAMPLE_ROWS = 8
NEG = -1e30
LANES = 128
VMEM_LIMIT_V7X = 56 * 1024 * 1024


def _mm(a, b):
    return jnp.dot(a, b, preferred_element_type=F32)


def _nt(a, b):
    return lax.dot_general(a, b, (((1,), (1,)), ((), ())), preferred_element_type=F32)


def _tn(a, b):
    return lax.dot_general(a, b, (((0,), (0,)), ((), ())), preferred_element_type=F32)


def _softplus(z):
    return jnp.maximum(z, 0.0) + jnp.log1p(jnp.exp(-jnp.abs(z)))


def _rms(x, w):
    return x * lax.rsqrt(jnp.mean(x * x, axis=-1, keepdims=True) + EPS) * w


def _rms_mod(x, w, shift, scale):
    return _rms(x, w) * (1.0 + scale) + shift


def _split3(x):
    p1 = x.astype(BF16)
    r1 = x - p1.astype(F32)
    p2 = r1.astype(BF16)
    p3 = (r1 - p2.astype(F32)).astype(BF16)
    return p1, p2, p3


def _params(*sem):
    return pltpu.CompilerParams(dimension_semantics=sem, vmem_limit_bytes=VMEM_LIMIT_V7X)


def _mod_spec(arr, tm, rows_per_group):
    g, r, d = arr.shape
    assert r == 1 or r == tm
    bpg = rows_per_group // tm if r == 1 else 1
    return pl.BlockSpec((None, r, d), lambda i, *_: (i // bpg, 0, 0))


def _ada_kernel(c_ref, w_ref, b_ref, o_ref):
    c = c_ref[...]
    s = (c * jax.nn.sigmoid(c)).astype(BF16)
    o_ref[...] = _mm(s, w_ref[...].astype(BF16)) + b_ref[...]


def _ada(c_all, w_ada, b_ada):
    depth, d, n = w_ada.shape
    rows = c_all.shape[0]
    tn = 1536
    return pl.pallas_call(
        _ada_kernel,
        grid=(depth, n // tn),
        in_specs=[pl.BlockSpec((rows, d), lambda l, j: (0, 0)),
                  pl.BlockSpec((None, d, tn), lambda l, j: (l, 0, j)),
                  pl.BlockSpec((None, 1, tn), lambda l, j: (l, 0, j))],
        out_specs=pl.BlockSpec((None, rows, tn), lambda l, j: (l, 0, j)),
        out_shape=jax.ShapeDtypeStruct((depth, rows, n), F32),
        compiler_params=_params("parallel", "parallel"),
        name="ada_mod",
    )(c_all, w_ada, b_ada.reshape(depth, 1, n))


def _proj_kernel(x_ref, nw_ref, sh_ref, sc_ref, w_ref, *refs, outs):
    o_refs = refs[len(refs) - len(outs):]
    h = _rms_mod(x_ref[...], nw_ref[...], sh_ref[...], sc_ref[...]).astype(BF16)
    products = {}
    for o_ref, (a, b, _, mult, stacked) in zip(o_refs, outs):
        if (a, b) not in products:
            products[(a, b)] = _mm(h, w_ref[:, a:b])
        y = products[(a, b)]
        if mult is not None:
            y = y * mult
        if stacked is not None:
            o_ref[...] = y.reshape(o_ref.shape).astype(o_ref.dtype)
        else:
            o_ref[...] = y.astype(o_ref.dtype)


def _proj(x, nw, shift, scale, w_all, layer, outs, tm, rows_per_group, name):
    n, d = x.shape
    out_specs, out_shape, stacked_in, aliases = [], [], [], {}
    for idx, (a, b, dt, _, stacked) in enumerate(outs):
        if stacked is not None:
            out_specs.append(pl.BlockSpec((None, tm) + stacked.shape[2:], lambda i: (layer, i, 0, 0)))
            out_shape.append(jax.ShapeDtypeStruct(stacked.shape, stacked.dtype))
            aliases[5 + len(stacked_in)] = idx
            stacked_in.append(stacked)
        else:
            out_specs.append(pl.BlockSpec((tm, b - a), lambda i: (i, 0)))
            out_shape.append(jax.ShapeDtypeStruct((n, b - a), dt))
    return pl.pallas_call(
        functools.partial(_proj_kernel, outs=tuple(o[:4] + (None if o[4] is None else True,) for o in outs)),
        grid=(n // tm,),
        in_specs=[pl.BlockSpec((tm, d), lambda i: (i, 0)),
                  pl.BlockSpec((1, d), lambda i: (0, 0)),
                  _mod_spec(shift, tm, rows_per_group),
                  _mod_spec(scale, tm, rows_per_group),
                  pl.BlockSpec((None,) + w_all.shape[1:], lambda i: (layer, 0, 0))]
                 + [pl.BlockSpec(memory_space=pl.ANY)] * len(stacked_in),
        out_specs=out_specs,
        out_shape=out_shape,
        input_output_aliases=aliases,
        compiler_params=_params("parallel"),
        name=name,
    )(x, nw.reshape(1, d), shift, scale, w_all, *stacked_in)


def _tail_kernel(a_ref, wo_ref, x_ref, g1_ref, nw_ref, sh_ref, sc_ref, g2_ref, wu_ref, wd_ref, fnw_ref,
                 o_ref, h_sc, acc_sc, *, final):
    f = pl.program_id(1)

    @pl.when(f == 0)
    def _():
        x1 = x_ref[...] + g1_ref[...] * _mm(a_ref[...].astype(BF16), wo_ref[...])
        o_ref[...] = x1
        h_sc[...] = _rms_mod(x1, nw_ref[...], sh_ref[...], sc_ref[...]).astype(BF16)
        acc_sc[...] = jnp.zeros_like(acc_sc)

    up = _mm(h_sc[...], wu_ref[...].astype(BF16))
    act = jnp.square(jnp.maximum(up, 0.0)).astype(BF16)
    acc_sc[...] += _mm(act, wd_ref[...].astype(BF16))

    @pl.when(f == pl.num_programs(1) - 1)
    def _():
        y = o_ref[...] + g2_ref[...] * acc_sc[...]
        if final:
            y = _rms(y, fnw_ref[...])
        o_ref[...] = y


def _tail(a, w_out_all, mix_layer, x, gate1, nw, shift, scale, gate2, w_up_all, w_down_all, layer, fnw, final,
          tm, tf, rows_per_group, name):
    n, d = x.shape
    dff = w_up_all.shape[2]
    rows = lambda i, f: (i, 0)
    const = lambda i, f: (0, 0)
    return pl.pallas_call(
        functools.partial(_tail_kernel, final=final),
        grid=(n // tm, dff // tf),
        in_specs=[pl.BlockSpec((tm, a.shape[1]), rows),
                  pl.BlockSpec((None,) + w_out_all.shape[1:], lambda i, f: (mix_layer, 0, 0)),
                  pl.BlockSpec((tm, d), rows),
                  _mod_spec(gate1, tm, rows_per_group),
                  pl.BlockSpec((1, d), const),
                  _mod_spec(shift, tm, rows_per_group),
                  _mod_spec(scale, tm, rows_per_group),
                  _mod_spec(gate2, tm, rows_per_group),
                  pl.BlockSpec((None, d, tf), lambda i, f: (layer, 0, f)),
                  pl.BlockSpec((None, tf, d), lambda i, f: (layer, f, 0)),
                  pl.BlockSpec((1, d), const)],
        out_specs=pl.BlockSpec((tm, d), rows),
        out_shape=jax.ShapeDtypeStruct((n, d), F32),
        scratch_shapes=[pltpu.VMEM((tm, d), BF16), pltpu.VMEM((tm, d), F32)],
        compiler_params=_params("parallel", "arbitrary"),
        name=name,
    )(a, w_out_all, x, gate1, nw.reshape(1, d), shift, scale, gate2, w_up_all, w_down_all, fnw.reshape(1, d))


def _mix_kernel(p_ref, g_ref, bif_ref, c0_ref, m0_ref, mhw_ref, sgnw_ref, sgw_ref, sgb_ref,
                hz_ref, rows_ref, cfin_ref, mfin_ref, c_sc, m_sc, *, block_rows, valid_len):
    L = MIX_CHUNK
    c = pl.program_id(1)

    @pl.when(c == 0)
    def _():
        c_sc[...] = c0_ref[...]
        m_sc[...] = m0_ref[...]

    rowi = lax.broadcasted_iota(jnp.int32, (L, LANES), 0)
    lanei = lax.broadcasted_iota(jnp.int32, (L, LANES), 1)
    is_i = lanei < A_HEADS
    is_f = (lanei >= A_HEADS) & (lanei < 2 * A_HEADS)
    sq_r = lax.broadcasted_iota(jnp.int32, (L, L), 0)
    sq_c = lax.broadcasted_iota(jnp.int32, (L, L), 1)
    causal = sq_c <= sq_r
    tril = jnp.where(causal, 1.0, 0.0).astype(BF16)
    eye = jnp.where(sq_c == sq_r, 1.0, 0.0).astype(BF16)
    ones = jnp.ones((L, LANES), F32)
    sel_r = lax.broadcasted_iota(jnp.int32, (LANES, 2 * LANES), 0)
    sel_c = lax.broadcasted_iota(jnp.int32, (LANES, 2 * LANES), 1)
    sel = [jnp.where(((sel_r == h) & (sel_c < LANES)) | ((sel_r == A_HEADS + h) & (sel_c >= LANES)), 1.0,
                     0.0).astype(BF16) for h in range(A_HEADS)]

    def chunk(p, g):
        gb = g + bif_ref[...]
        x8 = jnp.where(is_i, gb, jnp.where(is_f, -_softplus(-gb), 0.0))
        if valid_len < L:
            x8 = jnp.where(rowi < valid_len, x8, jnp.where(is_i, NEG, 0.0))

        parts = _split3(x8)
        bc = _mm(tril, parts[0]) + _mm(tril, parts[1]) + _mm(tril, parts[2])
        yparts = _split3(jnp.where(is_i, x8, bc))
        yparts_t = [_nt(eye, q) for q in yparts]
        y_t = yparts_t[0] + yparts_t[1] + yparts_t[2]

        has = []
        for h in range(A_HEADS):
            hp, hl = h // 2, h % 2
            headmask = (lanei >= A_DK * hl) & (lanei < A_DK * (hl + 1))
            q2 = p[:, hp * LANES:(hp + 1) * LANES]
            k2 = p[:, A_QK_W + hp * LANES:A_QK_W + (hp + 1) * LANES]
            v = p[:, 2 * A_QK_W + h * A_DV:2 * A_QK_W + (h + 1) * A_DV]
            qh = jnp.where(headmask, q2, 0.0).astype(BF16)
            kh = jnp.where(headmask, k2 * (A_DK ** -0.5), 0.0)
            rep = _mm(yparts[0], sel[h]) + _mm(yparts[1], sel[h]) + _mm(yparts[2], sel[h])
            ig_rep, b_rep = rep[:, :LANES], rep[:, LANES:]
            m_prev = m_sc[h]

            gk = jnp.where(causal, y_t[h:h + 1, :] - y_t[A_HEADS + h:A_HEADS + h + 1, :], NEG)
            mx = jnp.maximum(m_prev, jnp.max(gk, axis=1, keepdims=True))
            m_t = b_rep + mx
            w_st = jnp.exp(m_prev - mx)
            pmat = _nt(qh, kh.astype(BF16)) * jnp.exp(gk - mx)
            v_ext = jnp.concatenate([v, ones], axis=1).astype(BF16)
            c_old = c_sc[h]
            num_ext = _mm(pmat.astype(BF16), v_ext) + jnp.concatenate([w_st, w_st], axis=1) * _mm(
                qh, c_old.astype(BF16))
            hh = num_ext[:, :A_DV] / jnp.maximum(jnp.abs(num_ext[:, A_DV:]), jnp.exp(-m_t))

            m_new = m_t[L - 1:L, :]
            b_end = b_rep[L - 1:L, :]
            w_k = jnp.exp(b_end - b_rep + ig_rep - m_new)
            decay = jnp.exp(b_end + m_prev - m_new)
            c_sc[h] = jnp.concatenate([decay, decay], axis=1) * c_old + _tn((kh * w_k).astype(BF16), v_ext)
            m_sc[h] = m_new

            o = p[:, 2 * A_QK_W + A_WIDTH + h * A_DV:2 * A_QK_W + A_WIDTH + (h + 1) * A_DV]
            has.append(_rms(hh, mhw_ref[:, h * A_DV:(h + 1) * A_DV]) * jax.nn.sigmoid(o))

        u0 = 2 * A_QK_W + 2 * A_WIDTH
        zbs, vsn = [], []
        for gi in range(B_GROUPS):
            u = jax.nn.gelu(p[:, u0 + gi * B_GROUP_DIM:u0 + (gi + 1) * B_GROUP_DIM])
            vs = jax.nn.gelu(p[:, u0 + B_WIDTH + gi * B_GROUP_DIM:u0 + B_WIDTH + (gi + 1) * B_GROUP_DIM])
            vs = _rms(vs, sgnw_ref[:, gi * B_GROUP_DIM:(gi + 1) * B_GROUP_DIM])
            s = _mm(sgw_ref[gi], vs.astype(BF16)) + sgb_ref[gi]
            zbs.append(u * s)
            vsn.append(vs)
        return jnp.concatenate(has + zbs, axis=1), jnp.concatenate(vsn, axis=1)

    if block_rows < L:
        pad = lambda a: jnp.concatenate([a, jnp.zeros((L - block_rows, a.shape[1]), F32)], axis=0)
        hz, vsn = chunk(pad(p_ref[...]), pad(g_ref[...]))
        hz_ref[...] = hz[:block_rows].astype(hz_ref.dtype)
        rows_ref[...] = vsn[:block_rows]
    else:
        for ci in range(block_rows // L):
            sl = slice(ci * L, (ci + 1) * L)
            hz, vsn = chunk(p_ref[sl, :], g_ref[sl, :])
            hz_ref[sl, :] = hz.astype(hz_ref.dtype)
            rows_ref[sl, :] = vsn

    @pl.when(c == pl.num_programs(1) - 1)
    def _():
        cfin_ref[...] = c_sc[...]
        mfin_ref[...] = m_sc[...]


def _mix(p, g, b_if, c0, m0, mh_norm_w, sg_norm_w, sg_w_tril, sg_b_t, n_seq, block_rows, valid_len, name):
    n = p.shape[0]
    nc = n // (n_seq * block_rows)
    row_map = lambda b, c: (b * nc + c, 0)
    const2 = lambda b, c: (0, 0)
    return pl.pallas_call(
        functools.partial(_mix_kernel, block_rows=block_rows, valid_len=valid_len),
        grid=(n_seq, nc),
        in_specs=[pl.BlockSpec((block_rows, AB_MAIN), row_map),
                  pl.BlockSpec((block_rows, LANES), row_map),
                  pl.BlockSpec((1, LANES), const2),
                  pl.BlockSpec((None, A_HEADS, LANES, 2 * LANES), lambda b, c: (b, 0, 0, 0)),
                  pl.BlockSpec((None, A_HEADS, 1, LANES), lambda b, c: (b, 0, 0, 0)),
                  pl.BlockSpec((1, A_WIDTH), const2),
                  pl.BlockSpec((1, B_WIDTH), const2),
                  pl.BlockSpec((B_GROUPS, MIX_CHUNK, MIX_CHUNK), lambda b, c: (0, 0, 0)),
                  pl.BlockSpec((B_GROUPS, MIX_CHUNK, LANES), lambda b, c: (0, 0, 0))],
        out_specs=[pl.BlockSpec((block_rows, A_WIDTH + B_WIDTH), row_map),
                   pl.BlockSpec((block_rows, B_WIDTH), row_map),
                   pl.BlockSpec((None, A_HEADS, LANES, 2 * LANES), lambda b, c: (b, 0, 0, 0)),
                   pl.BlockSpec((None, A_HEADS, 1, LANES), lambda b, c: (b, 0, 0, 0))],
        out_shape=[jax.ShapeDtypeStruct((n, A_WIDTH + B_WIDTH), BF16),
                   jax.ShapeDtypeStruct((n, B_WIDTH), F32),
                   jax.ShapeDtypeStruct((n_seq, A_HEADS, LANES, 2 * LANES), F32),
                   jax.ShapeDtypeStruct((n_seq, A_HEADS, 1, LANES), F32)],
        scratch_shapes=[pltpu.VMEM((A_HEADS, LANES, 2 * LANES), F32), pltpu.VMEM((A_HEADS, 1, LANES), F32)],
        compiler_params=_params("parallel", "arbitrary"),
        name=name,
    )(p, g, b_if, c0, m0, mh_norm_w, sg_norm_w, sg_w_tril, sg_b_t)


def _pack_state(C, n, m):
    b = C.shape[0]
    cn = jnp.concatenate([C, jnp.broadcast_to(n[..., None], C.shape[:3] + (LANES,))], axis=-1)
    z = jnp.zeros_like(cn)
    cn = cn.reshape(b, A_HEADS // 2, 2, A_DK, 2 * LANES)
    z = z.reshape(cn.shape)
    even = jnp.concatenate([cn[:, :, 0], z[:, :, 0]], axis=-2)
    odd = jnp.concatenate([z[:, :, 1], cn[:, :, 1]], axis=-2)
    c0 = jnp.stack([even, odd], axis=2).reshape(b, A_HEADS, 2 * A_DK, 2 * LANES)
    return c0, jnp.broadcast_to(m[:, :, None, None], (b, A_HEADS, 1, LANES))


def _unpack_state(cfin, mfin):
    b = cfin.shape[0]
    c5 = cfin.reshape(b, A_HEADS // 2, 2, 2, A_DK, 2 * LANES)
    cn = jnp.stack([c5[:, :, 0, 0], c5[:, :, 1, 1]], axis=2).reshape(b, A_HEADS, A_DK, 2 * LANES)
    return cn[..., :A_DV], cn[..., A_DV], mfin[:, :, 0, 0]


def _strict_upper(tk):
    r = lax.broadcasted_iota(jnp.int32, (tk, tk), 0)
    c = lax.broadcasted_iota(jnp.int32, (tk, tk), 1)
    return jnp.where(r > c, 1.0, 0.0).astype(BF16)


def _attn_kernel(bias_ref, q_ref, k_ref, v_ref, o_ref, acc_sc, z0_sc, z1_sc, a0_sc, a1_sc, *, tq):
    tk = PROMPT_TK
    nd = tq // tk
    nsub = PROMPT_BLOCKS_PER_TRIP
    span = nsub * tk
    assert nd % nsub == 0
    h = pl.program_id(1)
    qi = pl.program_id(2)
    ntrip = qi * (nd // nsub)
    bias2 = bias_ref[h] * LOG2E
    nsu = -_strict_upper(tk)

    def trip_start(t):
        return pl.multiple_of(jnp.maximum(qi * tq - (t + 1) * span, 0), tk)

    def weights(z2, ncarry, mask):
        e = jnp.exp2(jnp.minimum(z2, EXP2_MAX))
        sp = jnp.log(1.0 + e)
        if mask is not None:
            sp = jnp.where(mask, sp, 0.0)
        n = z2.shape[1] // tk
        parts = [None] * n
        for i in reversed(range(n)):
            sp_i = sp[:, i * tk:(i + 1) * tk]
            nrest = _mm(sp_i.astype(BF16), nsu)
            parts[i] = e[:, i * tk:(i + 1) * tk] * jnp.exp(nrest + ncarry - sp_i)
            ncarry = ncarry + nrest[:, 0:1] - sp_i[:, 0:1]
        a = parts[0] if n == 1 else jnp.concatenate(parts, axis=1)
        if mask is not None:
            a = jnp.where(mask, a, 0.0)
        return a.astype(BF16), ncarry

    z0_sc[...] = _nt(q_ref[...], k_ref[pl.ds(trip_start(0), span), :])
    a1_sc[...] = jnp.zeros_like(a1_sc)

    row = lax.broadcasted_iota(jnp.int32, (tk, tk), 0)
    col = lax.broadcasted_iota(jnp.int32, (tk, tk), 1)
    ncarries = []
    for rb in range(nd):
        q_rows = q_ref[rb * tk:(rb + 1) * tk, :]
        ncarry = jnp.zeros((tk, 1), F32)
        acc = jnp.zeros((tk, C_HD), F32)
        for sub in reversed(range(rb + 1)):
            start = pl.multiple_of(qi * tq + sub * tk, tk)
            z2 = _nt(q_rows, k_ref[pl.ds(start, tk), :]) + bias2
            a, ncarry = weights(z2, ncarry, col < row if sub == rb else None)
            acc = acc + _mm(a, v_ref[pl.ds(start, tk), :])
        acc_sc[rb * tk:(rb + 1) * tk, :] = acc
        ncarries.append(ncarry)

    def stage(t, z_cur, z_nxt, a_prev, a_cur, ncarry):
        acc_sc[...] += _mm(a_prev[...], v_ref[pl.ds(trip_start(t - 1), span), :])
        z_nxt[...] = _nt(q_ref[...], k_ref[pl.ds(trip_start(t + 1), span), :])
        a, ncarry = weights(z_cur[...] + bias2, ncarry, None)
        a_cur[...] = a
        return ncarry

    def two_trips(i, ncarry):
        ncarry = stage(2 * i, z0_sc, z1_sc, a1_sc, a0_sc, ncarry)
        return stage(2 * i + 1, z1_sc, z0_sc, a0_sc, a1_sc, ncarry)

    ncarry = lax.fori_loop(0, ntrip // 2, two_trips, jnp.concatenate(ncarries, axis=0))

    @pl.when(ntrip % 2 == 1)
    def _():
        stage(ntrip - 1, z0_sc, z1_sc, a1_sc, a0_sc, ncarry)
        acc_sc[...] += _mm(a0_sc[...], v_ref[pl.ds(trip_start(ntrip - 1), span), :])

    @pl.when(ntrip % 2 == 0)
    def _():
        acc_sc[...] += _mm(a1_sc[...], v_ref[pl.ds(trip_start(ntrip - 1), span), :])

    o_ref[...] = acc_sc[...].astype(o_ref.dtype)


def _attn(q, k, v, bias, n_seq, tq):
    n = q.shape[0]
    t = n // n_seq
    nq = t // tq
    span = PROMPT_BLOCKS_PER_TRIP * PROMPT_TK
    assert tq % span == 0
    return pl.pallas_call(
        functools.partial(_attn_kernel, tq=tq),
        grid=(n_seq, C_HEADS, nq),
        in_specs=[pl.BlockSpec(memory_space=pltpu.SMEM),
                  pl.BlockSpec((tq, C_HD), lambda b, h, i: (b * nq + i, h)),
                  pl.BlockSpec((t, C_HD), lambda b, h, i: (b, h)),
                  pl.BlockSpec((t, C_HD), lambda b, h, i: (b, h))],
        out_specs=pl.BlockSpec((tq, C_HD), lambda b, h, i: (b * nq + i, h)),
        out_shape=jax.ShapeDtypeStruct((n, C_WIDTH), BF16),
        scratch_shapes=[pltpu.VMEM((tq, C_HD), F32)]
                       + [pltpu.VMEM((tq, span), F32)] * 2 + [pltpu.VMEM((tq, span), BF16)] * 2,
        compiler_params=_params("parallel", "parallel", "arbitrary"),
        name="sb_attention",
    )(bias, q, k, v)


def _decode_kernel(pt_ref, q_ref, kn_ref, vn_ref, *refs, n_new, n_pg):
    del pt_ref
    kc_refs, vc_refs = refs[:n_pg], refs[n_pg:2 * n_pg]
    bias_ref, o_ref, acc_sc, carry_sc = refs[2 * n_pg:]
    tk = ATT_TK
    R = SAMPLE_ROWS
    HR = C_HEADS * R
    j = pl.program_id(1)
    su = _strict_upper(tk)

    def weights(z, mask, carry_in):
        sp = _softplus(z)
        if mask is not None:
            sp = jnp.where(mask, sp, 0.0)
        hi = sp.astype(BF16)
        lo = (sp - hi.astype(F32)).astype(BF16)
        rest = _mm(hi, su) + _mm(lo, su)
        tot = rest[:, 0:1] + sp[:, 0:1]
        carries = [carry_in]
        for p in range(z.shape[0] // HR):
            carries.append(carries[-1] + tot[p * HR:(p + 1) * HR])
        a = jnp.exp(z - sp - rest - jnp.concatenate(carries[:-1], axis=0))
        if mask is not None:
            a = jnp.where(mask, a, 0.0)
        return a, carries[-1]

    @pl.when(j == 0)
    def _():
        pad = jnp.zeros((tk - R, C_HD), F32)
        zs, vals = [], []
        for h in range(C_HEADS):
            sl = slice(h * C_HD, (h + 1) * C_HD)
            kh = jnp.concatenate([kn_ref[:, sl], pad], axis=0).astype(BF16)
            vals.append(jnp.concatenate([vn_ref[:, sl], pad], axis=0).astype(BF16))
            zs.append(_nt(q_ref[:, sl].astype(BF16), kh))
        z = jnp.concatenate(zs, axis=0) * SB_SCALE + bias_ref[...]
        row = lax.broadcasted_iota(jnp.int32, z.shape, 0)
        col = lax.broadcasted_iota(jnp.int32, z.shape, 1)
        a, carry = weights(z, (col < row % R) & (col < n_new), jnp.zeros((HR, 1), F32))
        for h in range(C_HEADS):
            acc_sc[h * R:(h + 1) * R, :] = _mm(a[h * R:(h + 1) * R].astype(BF16), vals[h])
        carry_sc[...] = jnp.broadcast_to(carry, carry_sc.shape)

    zs = [[None] * C_HEADS for _ in range(n_pg)]
    vals = []
    for h in range(C_HEADS):
        sl = slice(h * C_HD, (h + 1) * C_HD)
        kh = jnp.concatenate([r[pl.ds(h, tk, stride=C_HEADS), :].astype(BF16) for r in kc_refs], axis=0)
        vals.append(jnp.concatenate([r[pl.ds(h, tk, stride=C_HEADS), :].astype(BF16) for r in vc_refs], axis=0))
        zh = _nt(q_ref[:, sl].astype(BF16), kh)
        for p in range(n_pg):
            zs[p][h] = zh[:, p * tk:(p + 1) * tk]
    z = jnp.concatenate([zs[p][h] for p in range(n_pg) for h in range(C_HEADS)], axis=0)
    z = z * SB_SCALE + jnp.concatenate([bias_ref[...]] * n_pg, axis=0)
    a, carry = weights(z, None, carry_sc[:, 0:1])
    for h in range(C_HEADS):
        ah = jnp.concatenate([a[(p * C_HEADS + h) * R:(p * C_HEADS + h + 1) * R] for p in range(n_pg)], axis=1)
        acc_sc[h * R:(h + 1) * R, :] += _mm(ah.astype(BF16), vals[h])
    carry_sc[...] = jnp.broadcast_to(carry, carry_sc.shape)

    @pl.when(j == pl.num_programs(1) - 1)
    def _():
        for h in range(C_HEADS):
            o_ref[:, h * C_HD:(h + 1) * C_HD] = acc_sc[h * R:(h + 1) * R, :]


def _decode_attn(q, k_new, v_new, cache_k, cache_v, page_table, bias_rows, layer, n_new):
    n_seq, n_pages = page_table.shape
    n_phys, n_att, page, heads, hd = cache_k.shape
    n_pg = DECODE_PAGES
    assert n_pages % n_pg == 0 and page == ATT_TK
    ck = cache_k.reshape(n_phys, n_att, page * heads, hd)
    cv = cache_v.reshape(n_phys, n_att, page * heads, hd)
    R = SAMPLE_ROWS
    row_map = lambda b, j, pt: (b, 0)

    def page_spec(p):
        return pl.BlockSpec((None, None, page * heads, hd),
                            lambda b, j, pt: (pt[b, n_pages - 1 - (j * n_pg + p)], layer, 0, 0))

    grid_spec = pltpu.PrefetchScalarGridSpec(
        num_scalar_prefetch=1,
        grid=(n_seq, n_pages // n_pg),
        in_specs=[pl.BlockSpec((R, C_WIDTH), row_map)] * 3
                 + [page_spec(p) for p in range(n_pg)] * 2
                 + [pl.BlockSpec((C_HEADS * R, LANES), lambda b, j, pt: (0, 0))],
        out_specs=pl.BlockSpec((R, C_WIDTH), row_map),
        scratch_shapes=[pltpu.VMEM((C_HEADS * R, C_HD), F32), pltpu.VMEM((C_HEADS * R, LANES), F32)],
    )
    return pl.pallas_call(
        functools.partial(_decode_kernel, n_new=n_new, n_pg=n_pg),
        grid_spec=grid_spec,
        out_shape=jax.ShapeDtypeStruct((n_seq * R, C_WIDTH), F32),
        compiler_params=_params("parallel", "arbitrary"),
        name="sb_decode",
    )(page_table, q, k_new, v_new, *([ck] * n_pg), *([cv] * n_pg), bias_rows)


def _trunk(x, mods, W, n_seq, rows_per_seq, tm, state, valid_len, past):
    n = x.shape[0]
    depth = W['w_up'].shape[0]
    n_att = W['w_qkv_c'].shape[0]
    block_rows = min(rows_per_seq, MIX_CHUNK * MIX_CHUNKS_PER_STEP)
    rec, rows, ks, vs = [], [], [], []
    if past is None:
        ks = jnp.zeros((n_att, n, C_HEADS, C_HD), F32)
        vs = jnp.zeros((n_att, n, C_HEADS, C_HD), F32)
    for l in range(depth):
        sh1, sc1, g1, sh2, sc2, g2 = mods[l]
        if l % 2 == 0:
            e = l // 2
            p, gates = _proj(x, W['norm_w'][l, 0], sh1, sc1, W['w_in_ab'], e,
                             [(0, AB_MAIN, F32, None, None), (AB_MAIN, AB_MAIN + LANES, F32, None, None)],
                             tm, rows_per_seq, "proj_even")
            c0, m0 = _pack_state(*state(e))
            mixed, r, cfin, mfin = _mix(p, gates, W['b_if'][e], c0, m0, W['mh_norm_w'][e], W['sg_norm_w'][e],
                                        W['sg_w'][e], W['sg_b'][e], n_seq, block_rows, valid_len, "mix_even")
            rec.append(_unpack_state(cfin, mfin))
            rows.append(r)
        else:
            o = l // 2
            kcols, vcols = (C_WIDTH, 2 * C_WIDTH), (2 * C_WIDTH, 3 * C_WIDTH)
            if past is None:
                q, ks, vs, kb, vb = _proj(x, W['norm_w'][l, 0], sh1, sc1, W['w_qkv_c'], o,
                                          [(0, C_WIDTH, BF16, SB_SCALE * LOG2E, None),
                                           kcols + (F32, None, ks), vcols + (F32, None, vs),
                                           kcols + (BF16, None, None), vcols + (BF16, None, None)],
                                          tm, rows_per_seq, "proj_odd")
                mixed = _attn(q, kb, vb, W['b_sb'][o], n_seq, min(PROMPT_TQ, rows_per_seq))
            else:
                q, k, v = _proj(x, W['norm_w'][l, 0], sh1, sc1, W['w_qkv_c'], o,
                                [(0, C_WIDTH, F32, None, None), kcols + (F32, None, None),
                                 vcols + (F32, None, None)], tm, rows_per_seq, "proj_odd_s")
                bias_rows = jnp.broadcast_to(jnp.repeat(W['b_sb'][o], SAMPLE_ROWS)[:, None],
                                             (C_HEADS * SAMPLE_ROWS, LANES))
                mixed = _decode_attn(q, k, v, past[0], past[1], past[2], bias_rows, o, valid_len)
                ks.append(k)
                vs.append(v)
        x = _tail(mixed, W['w_out'], l, x, g1, W['norm_w'][l, 1], sh2, sc2, g2, W['w_up'], W['w_down'], l,
                  W['final_norm_w'], l == depth - 1, min(MLP_TM, n), MLP_TF, rows_per_seq, "tail")
    return x, rec, rows, ks, vs


def kernel(x_prompt, x_sample, state_mlstm_C, state_mlstm_n, state_mlstm_m, cache_k, cache_v, page_table, c_prompt, c_sample, w_ada, b_ada, norm_w, final_norm_w, w_in_ab, b_if, mh_norm_w, sg_norm_w, sg_w, sg_b, w_out_ab, w_qkv_c, b_sb, w_out_c, w_up, w_down):
    bp, t_p, d = x_prompt.shape
    bs, t_s, _ = x_sample.shape
    depth = w_ada.shape[0]
    n_rec = w_in_ab.shape[0]
    R = SAMPLE_ROWS
    assert t_s <= R and t_p % MIX_CHUNK == 0

    w_in = jnp.concatenate([w_in_ab[:, :, :AB_MAIN], w_in_ab[:, :, AB_MAIN:],
                            jnp.zeros((n_rec, d, LANES - 2 * A_HEADS), F32)], axis=-1).astype(BF16)
    causal = jnp.tril(jnp.ones((MIX_CHUNK, MIX_CHUNK), bool))
    W = dict(
        norm_w=norm_w, final_norm_w=final_norm_w, w_in_ab=w_in,
        b_if=jnp.concatenate([b_if, jnp.zeros((n_rec, LANES - 2 * A_HEADS), F32)], axis=-1).reshape(n_rec, 1, LANES),
        mh_norm_w=mh_norm_w.reshape(n_rec, 1, A_WIDTH), sg_norm_w=sg_norm_w.reshape(n_rec, 1, B_WIDTH),
        sg_w=jnp.where(causal, sg_w, 0.0).astype(BF16),
        sg_b=jnp.broadcast_to(sg_b[:, :, :MIX_CHUNK, None], (n_rec, B_GROUPS, MIX_CHUNK, LANES)),
        w_out=jnp.stack([(w_out_ab if l % 2 == 0 else w_out_c)[l // 2] for l in range(depth)]).astype(BF16),
        w_qkv_c=w_qkv_c.astype(BF16), b_sb=b_sb, w_up=w_up, w_down=w_down)

    n_c = bp + bs
    n_c_pad = -(-n_c // 8) * 8
    c_all = jnp.concatenate([c_prompt, c_sample, jnp.zeros((n_c_pad - n_c, d), F32)], axis=0)
    mod = _ada(c_all, w_ada, b_ada).reshape(depth, n_c_pad, N_MOD, d)
    mods_p = [[mod[l, :bp, i].reshape(bp, 1, d) for i in range(N_MOD)] for l in range(depth)]
    mod_s = jnp.repeat(mod[:, bp:bp + bs], R, axis=1)
    mods_s = [[mod_s[l, :, i].reshape(1, bs * R, d) for i in range(N_MOD)] for l in range(depth)]

    zero_state = (jnp.zeros((bp, A_HEADS, A_DK, A_DV), F32), jnp.zeros((bp, A_HEADS, A_DK), F32),
                  jnp.zeros((bp, A_HEADS), F32))
    yp, rec_p, _, ks_p, vs_p = _trunk(x_prompt.reshape(bp * t_p, d), mods_p, W, bp, t_p, 512,
                                      lambda e: zero_state, MIX_CHUNK, None)

    xs = jnp.pad(x_sample, ((0, 0), (0, R - t_s), (0, 0))).reshape(bs * R, d)
    ys, rec_s, rows_s, ks_s, vs_s = _trunk(
        xs, mods_s, W, bs, R, bs * R,
        lambda e: (state_mlstm_C[e], state_mlstm_n[e], state_mlstm_m[e]), t_s, (cache_k, cache_v, page_table))

    def unpad(a, tail):
        return a.reshape((bs, R) + tail)[:, :t_s]

    return (yp.reshape(bp, t_p, d), unpad(ys, (d,)),
            jnp.stack([r[0] for r in rec_p]), jnp.stack([r[1] for r in rec_p]), jnp.stack([r[2] for r in rec_p]),
            jnp.stack([r[0] for r in rec_s]), jnp.stack([r[1] for r in rec_s]), jnp.stack([r[2] for r in rec_s]),
            jnp.stack([unpad(r, (B_WIDTH,)) for r in rows_s]),
            ks_p.reshape(-1, bp, t_p, C_HEADS, C_HD), vs_p.reshape(-1, bp, t_p, C_HEADS, C_HD),
            jnp.stack([unpad(k, (C_HEADS, C_HD)) for k in ks_s]),
            jnp.stack([unpad(v, (C_HEADS, C_HD)) for v in vs_s]))
```

```python
import functools

import jax
import jax.numpy as jnp
from jax import lax
from jax.experimental import pallas as pl
from jax.experimental.pallas import tpu as pltpu

F32 = jnp.float32
BF16 = jnp.bfloat16

D_MODEL = 1024
A_HEADS = 4
A_DV = 128
A_DK = 64
A_WIDTH = A_HEADS * A_DV
A_QK_W = A_HEADS * A_DK
B_GROUPS = 4
B_WIDTH = 512
B_GROUP_DIM = 128
C_HEADS = 8
C_HD = 128
C_WIDTH = C_HEADS * C_HD
SB_SCALE = C_HD ** -0.5
D_FF = 4 * D_MODEL
N_MOD = 6
EPS = 1e-6
AB_MAIN = 2 * A_QK_W + 2 * A_WIDTH + 2 * B_WIDTH
MIX_CHUNK = 128
MIX_CHUNKS_PER_STEP = 4
ATT_TK = 128
PROMPT_TK = 256
PROMPT_TQ = 512
PROMPT_BLOCKS_PER_TRIP = 2
DECODE_PAGES = 16
MLP_TM = 1024
MLP_TF = 512
ROW_SLAB = 256
LOG2E = 1.4426950408889634
EXP2_MAX = 126.0
SAMPLE_ROWS = 8
NEG = -1e30
LANES = 128
VMEM_LIMIT_V7X = 56 * 1024 * 1024


def _mm(a, b):
    return jnp.dot(a, b, preferred_element_type=F32)


def _nt(a, b):
    return lax.dot_general(a, b, (((1,), (1,)), ((), ())), preferred_element_type=F32)


def _tn(a, b):
    return lax.dot_general(a, b, (((0,), (0,)), ((), ())), preferred_element_type=F32)


def _softplus(z):
    return jnp.maximum(z, 0.0) + jnp.log1p(jnp.exp(-jnp.abs(z)))


def _rms(x, w):
    return x * lax.rsqrt(jnp.mean(x * x, axis=-1, keepdims=True) + EPS) * w


def _rms_mod(x, w, shift, scale):
    return _rms(x, w) * (1.0 + scale) + shift


def _split3(x):
    p1 = x.astype(BF16)
    r1 = x - p1.astype(F32)
    p2 = r1.astype(BF16)
    p3 = (r1 - p2.astype(F32)).astype(BF16)
    return p1, p2, p3


def _params(*sem):
    return pltpu.CompilerParams(dimension_semantics=sem, vmem_limit_bytes=VMEM_LIMIT_V7X)


def _mod_spec(mod, tm, rows_per_group):
    arr, base = mod
    g, r, d = arr.shape
    assert r == 1 or r == tm
    if r == 1:
        bpg = rows_per_group // tm
        return pl.BlockSpec((None, r, d), lambda i, *_: (base + i // bpg, 0, 0))
    return pl.BlockSpec((None, r, d), lambda i, *_: (base, 0, 0))


def _ada_kernel(c_ref, w_ref, b_ref, o_ref):
    c = c_ref[...]
    s = (c * jax.nn.sigmoid(c)).astype(BF16)
    o_ref[...] = _mm(s, w_ref[...].astype(BF16)) + b_ref[...]


def _ada(c_all, w_ada, b_ada):
    depth, d, n = w_ada.shape
    rows = c_all.shape[0]
    tn = 1536
    return pl.pallas_call(
        _ada_kernel,
        grid=(depth, n // tn),
        in_specs=[pl.BlockSpec((rows, d), lambda l, j: (0, 0)),
                  pl.BlockSpec((None, d, tn), lambda l, j: (l, 0, j)),
                  pl.BlockSpec((None, 1, tn), lambda l, j: (l, 0, j))],
        out_specs=pl.BlockSpec((None, rows, tn), lambda l, j: (l, 0, j)),
        out_shape=jax.ShapeDtypeStruct((depth, rows, n), F32),
        compiler_params=_params("parallel", "parallel"),
        name="ada_mod",
    )(c_all, w_ada, b_ada.reshape(depth, 1, n))


def _proj_kernel(x_ref, nw_ref, sh_ref, sc_ref, w_ref, *refs, outs):
    o_refs = refs[len(refs) - len(outs):]
    tm = x_ref.shape[0]
    rs = min(tm, ROW_SLAB)
    for r0 in range(0, tm, rs):
        rows = slice(r0, r0 + rs)
        mod = lambda ref: ref[...] if ref.shape[0] == 1 else ref[rows, :]
        h = _rms_mod(x_ref[rows, :], nw_ref[...], mod(sh_ref), mod(sc_ref)).astype(BF16)
        products = {}
        for o_ref, (a, b, _, mult, stacked) in zip(o_refs, outs):
            if (a, b) not in products:
                products[(a, b)] = _mm(h, w_ref[:, a:b])
            y = products[(a, b)]
            if mult is not None:
                y = y * mult
            if stacked == "update":
                o_ref[rows] = y.reshape((rs,) + o_ref.shape[1:]).astype(o_ref.dtype)
            elif stacked is not None:
                slot, n_slots = stacked
                for s in range(n_slots):
                    val = y.reshape((rs,) + o_ref.shape[2:]) if s == slot else jnp.zeros((rs,) + o_ref.shape[2:], F32)
                    o_ref[s, rows] = val.astype(o_ref.dtype)
            else:
                o_ref[rows, :] = y.astype(o_ref.dtype)


def _proj(x, nw, shift, scale, w_all, layer, outs, tm, rows_per_group, name):
    n, d = x.shape
    out_specs, out_shape, stacked_in, aliases, modes = [], [], [], {}, []
    for idx, (a, b, dt, _, stacked) in enumerate(outs):
        heads = (b - a) // LANES
        if stacked is None:
            out_specs.append(pl.BlockSpec((tm, b - a), lambda i: (i, 0)))
            out_shape.append(jax.ShapeDtypeStruct((n, b - a), dt))
            modes.append(None)
        elif isinstance(stacked, int):
            out_specs.append(pl.BlockSpec((stacked, tm, heads, LANES), lambda i: (0, i, 0, 0)))
            out_shape.append(jax.ShapeDtypeStruct((stacked, n, heads, LANES), dt))
            modes.append((layer, stacked))
        else:
            out_specs.append(pl.BlockSpec((None, tm) + stacked.shape[2:], lambda i: (layer, i, 0, 0)))
            out_shape.append(jax.ShapeDtypeStruct(stacked.shape, stacked.dtype))
            aliases[5 + len(stacked_in)] = idx
            stacked_in.append(stacked)
            modes.append("update")
    return pl.pallas_call(
        functools.partial(_proj_kernel, outs=tuple(o[:4] + (m,) for o, m in zip(outs, modes))),
        grid=(n // tm,),
        in_specs=[pl.BlockSpec((tm, d), lambda i: (i, 0)),
                  pl.BlockSpec((1, d), lambda i: (0, 0)),
                  _mod_spec(shift, tm, rows_per_group),
                  _mod_spec(scale, tm, rows_per_group),
                  pl.BlockSpec((None,) + w_all.shape[1:], lambda i: (layer, 0, 0))]
                 + [pl.BlockSpec(memory_space=pl.ANY)] * len(stacked_in),
        out_specs=out_specs,
        out_shape=out_shape,
        input_output_aliases=aliases,
        compiler_params=_params("parallel"),
        name=name,
    )(x, nw.reshape(1, d), shift[0], scale[0], w_all, *stacked_in)


def _tail_kernel(a_ref, wo_ref, x_ref, g1_ref, nw_ref, sh_ref, sc_ref, g2_ref, wu_ref, wd_ref, fnw_ref,
                 o_ref, h_sc, acc_sc, *, final):
    f = pl.program_id(1)

    @pl.when(f == 0)
    def _():
        tm = x_ref.shape[0]
        rs = min(tm, ROW_SLAB)
        for r0 in range(0, tm, rs):
            rows = slice(r0, r0 + rs)
            mod = lambda ref: ref[...] if ref.shape[0] == 1 else ref[rows, :]
            x1 = x_ref[rows, :] + mod(g1_ref) * _mm(a_ref[rows, :].astype(BF16), wo_ref[...])
            o_ref[rows, :] = x1
            h_sc[rows, :] = _rms_mod(x1, nw_ref[...], mod(sh_ref), mod(sc_ref)).astype(BF16)
        acc_sc[...] = jnp.zeros_like(acc_sc)

    up = _mm(h_sc[...], wu_ref[...].astype(BF16))
    act = jnp.square(jnp.maximum(up, 0.0)).astype(BF16)
    acc_sc[...] += _mm(act, wd_ref[...].astype(BF16))

    @pl.when(f == pl.num_programs(1) - 1)
    def _():
        y = o_ref[...] + g2_ref[...] * acc_sc[...]
        if final:
            y = _rms(y, fnw_ref[...])
        o_ref[...] = y


def _tail(a, w_out_all, mix_layer, x, gate1, nw, shift, scale, gate2, w_up_all, w_down_all, layer, fnw, final,
          tm, tf, rows_per_group, name):
    n, d = x.shape
    dff = w_up_all.shape[2]
    rows = lambda i, f: (i, 0)
    const = lambda i, f: (0, 0)
    return pl.pallas_call(
        functools.partial(_tail_kernel, final=final),
        grid=(n // tm, dff // tf),
        in_specs=[pl.BlockSpec((tm, a.shape[1]), rows),
                  pl.BlockSpec((None,) + w_out_all.shape[1:], lambda i, f: (mix_layer, 0, 0)),
                  pl.BlockSpec((tm, d), rows),
                  _mod_spec(gate1, tm, rows_per_group),
                  pl.BlockSpec((1, d), const),
                  _mod_spec(shift, tm, rows_per_group),
                  _mod_spec(scale, tm, rows_per_group),
                  _mod_spec(gate2, tm, rows_per_group),
                  pl.BlockSpec((None, d, tf), lambda i, f: (layer, 0, f)),
                  pl.BlockSpec((None, tf, d), lambda i, f: (layer, f, 0)),
                  pl.BlockSpec((1, d), const)],
        out_specs=pl.BlockSpec((tm, d), rows),
        out_shape=jax.ShapeDtypeStruct((n, d), F32),
        scratch_shapes=[pltpu.VMEM((tm, d), BF16), pltpu.VMEM((tm, d), F32)],
        compiler_params=_params("parallel", "arbitrary"),
        name=name,
    )(a, w_out_all, x, gate1[0], nw.reshape(1, d), shift[0], scale[0], gate2[0], w_up_all, w_down_all,
      fnw.reshape(1, d))


def _mix_kernel(p_ref, g_ref, bif_ref, c0_ref, m0_ref, mhw_ref, sgnw_ref, sgw_ref, sgb_ref,
                hz_ref, rows_ref, cfin_ref, mfin_ref, c_sc, m_sc, *, block_rows, valid_len):
    L = MIX_CHUNK
    c = pl.program_id(1)

    @pl.when(c == 0)
    def _():
        c_sc[...] = c0_ref[...]
        m_sc[...] = m0_ref[...]

    rowi = lax.broadcasted_iota(jnp.int32, (L, LANES), 0)
    lanei = lax.broadcasted_iota(jnp.int32, (L, LANES), 1)
    is_i = lanei < A_HEADS
    is_f = (lanei >= A_HEADS) & (lanei < 2 * A_HEADS)
    sq_r = lax.broadcasted_iota(jnp.int32, (L, L), 0)
    sq_c = lax.broadcasted_iota(jnp.int32, (L, L), 1)
    causal = sq_c <= sq_r
    tril = jnp.where(causal, 1.0, 0.0).astype(BF16)
    eye = jnp.where(sq_c == sq_r, 1.0, 0.0).astype(BF16)
    ones = jnp.ones((L, LANES), F32)
    sel_r = lax.broadcasted_iota(jnp.int32, (LANES, 2 * LANES), 0)
    sel_c = lax.broadcasted_iota(jnp.int32, (LANES, 2 * LANES), 1)
    sel = [jnp.where(((sel_r == h) & (sel_c < LANES)) | ((sel_r == A_HEADS + h) & (sel_c >= LANES)), 1.0,
                     0.0).astype(BF16) for h in range(A_HEADS)]

    def chunk(p, g):
        gb = g + bif_ref[...]
        x8 = jnp.where(is_i, gb, jnp.where(is_f, -_softplus(-gb), 0.0))
        if valid_len < L:
            x8 = jnp.where(rowi < valid_len, x8, jnp.where(is_i, NEG, 0.0))

        parts = _split3(x8)
        bc = _mm(tril, parts[0]) + _mm(tril, parts[1]) + _mm(tril, parts[2])
        yparts = _split3(jnp.where(is_i, x8, bc))
        yparts_t = [_nt(eye, q) for q in yparts]
        y_t = yparts_t[0] + yparts_t[1] + yparts_t[2]

        has = []
        for h in range(A_HEADS):
            hp, hl = h // 2, h % 2
            headmask = (lanei >= A_DK * hl) & (lanei < A_DK * (hl + 1))
            q2 = p[:, hp * LANES:(hp + 1) * LANES]
            k2 = p[:, A_QK_W + hp * LANES:A_QK_W + (hp + 1) * LANES]
            v = p[:, 2 * A_QK_W + h * A_DV:2 * A_QK_W + (h + 1) * A_DV]
            qh = jnp.where(headmask, q2, 0.0).astype(BF16)
            kh = jnp.where(headmask, k2 * (A_DK ** -0.5), 0.0)
            rep = _mm(yparts[0], sel[h]) + _mm(yparts[1], sel[h]) + _mm(yparts[2], sel[h])
            ig_rep, b_rep = rep[:, :LANES], rep[:, LANES:]
            m_prev = m_sc[h]

            gk = jnp.where(causal, y_t[h:h + 1, :] - y_t[A_HEADS + h:A_HEADS + h + 1, :], NEG)
            mx = jnp.maximum(m_prev, jnp.max(gk, axis=1, keepdims=True))
            m_t = b_rep + mx
            w_st = jnp.exp(m_prev - mx)
            pmat = _nt(qh, kh.astype(BF16)) * jnp.exp(gk - mx)
            v_ext = jnp.concatenate([v, ones], axis=1).astype(BF16)
            c_old = c_sc[h]
            num_ext = _mm(pmat.astype(BF16), v_ext) + jnp.concatenate([w_st, w_st], axis=1) * _mm(
                qh, c_old.astype(BF16))
            hh = num_ext[:, :A_DV] / jnp.maximum(jnp.abs(num_ext[:, A_DV:]), jnp.exp(-m_t))

            m_new = m_t[L - 1:L, :]
            b_end = b_rep[L - 1:L, :]
            w_k = jnp.exp(b_end - b_rep + ig_rep - m_new)
            decay = jnp.exp(b_end + m_prev - m_new)
            c_sc[h] = jnp.concatenate([decay, decay], axis=1) * c_old + _tn((kh * w_k).astype(BF16), v_ext)
            m_sc[h] = m_new

            o = p[:, 2 * A_QK_W + A_WIDTH + h * A_DV:2 * A_QK_W + A_WIDTH + (h + 1) * A_DV]
            has.append(_rms(hh, mhw_ref[:, h * A_DV:(h + 1) * A_DV]) * jax.nn.sigmoid(o))

        u0 = 2 * A_QK_W + 2 * A_WIDTH
        zbs, vsn = [], []
        for gi in range(B_GROUPS):
            u = jax.nn.gelu(p[:, u0 + gi * B_GROUP_DIM:u0 + (gi + 1) * B_GROUP_DIM])
            vs = jax.nn.gelu(p[:, u0 + B_WIDTH + gi * B_GROUP_DIM:u0 + B_WIDTH + (gi + 1) * B_GROUP_DIM])
            vs = _rms(vs, sgnw_ref[:, gi * B_GROUP_DIM:(gi + 1) * B_GROUP_DIM])
            s = _mm(sgw_ref[gi], vs.astype(BF16)) + sgb_ref[gi]
            zbs.append(u * s)
            vsn.append(vs)
        return jnp.concatenate(has + zbs, axis=1), jnp.concatenate(vsn, axis=1)

    if block_rows < L:
        pad = lambda a: jnp.concatenate([a, jnp.zeros((L - block_rows, a.shape[1]), F32)], axis=0)
        hz, vsn = chunk(pad(p_ref[...]), pad(g_ref[...]))
        hz_ref[...] = hz[:block_rows].astype(hz_ref.dtype)
        rows_ref[...] = vsn[:block_rows]
    else:
        for ci in range(block_rows // L):
            sl = slice(ci * L, (ci + 1) * L)
            hz, vsn = chunk(p_ref[sl, :], g_ref[sl, :])
            hz_ref[sl, :] = hz.astype(hz_ref.dtype)
            rows_ref[sl, :] = vsn

    @pl.when(c == pl.num_programs(1) - 1)
    def _():
        cfin_ref[...] = c_sc[...]
        mfin_ref[...] = m_sc[...]


def _mix(p, g, b_if, c0, m0, mh_norm_w, sg_norm_w, sg_w_tril, sg_b_t, n_seq, block_rows, valid_len, name):
    n = p.shape[0]
    nc = n // (n_seq * block_rows)
    row_map = lambda b, c: (b * nc + c, 0)
    const2 = lambda b, c: (0, 0)
    return pl.pallas_call(
        functools.partial(_mix_kernel, block_rows=block_rows, valid_len=valid_len),
        grid=(n_seq, nc),
        in_specs=[pl.BlockSpec((block_rows, AB_MAIN), row_map),
                  pl.BlockSpec((block_rows, LANES), row_map),
                  pl.BlockSpec((1, LANES), const2),
                  pl.BlockSpec((None, A_HEADS, LANES, 2 * LANES), lambda b, c: (b, 0, 0, 0)),
                  pl.BlockSpec((None, A_HEADS, 1, LANES), lambda b, c: (b, 0, 0, 0)),
                  pl.BlockSpec((1, A_WIDTH), const2),
                  pl.BlockSpec((1, B_WIDTH), const2),
                  pl.BlockSpec((B_GROUPS, MIX_CHUNK, MIX_CHUNK), lambda b, c: (0, 0, 0)),
                  pl.BlockSpec((B_GROUPS, MIX_CHUNK, LANES), lambda b, c: (0, 0, 0))],
        out_specs=[pl.BlockSpec((block_rows, A_WIDTH + B_WIDTH), row_map),
                   pl.BlockSpec((block_rows, B_WIDTH), row_map),
                   pl.BlockSpec((None, A_HEADS, LANES, 2 * LANES), lambda b, c: (b, 0, 0, 0)),
                   pl.BlockSpec((None, A_HEADS, 1, LANES), lambda b, c: (b, 0, 0, 0))],
        out_shape=[jax.ShapeDtypeStruct((n, A_WIDTH + B_WIDTH), BF16),
                   jax.ShapeDtypeStruct((n, B_WIDTH), F32),
                   jax.ShapeDtypeStruct((n_seq, A_HEADS, LANES, 2 * LANES), F32),
                   jax.ShapeDtypeStruct((n_seq, A_HEADS, 1, LANES), F32)],
        scratch_shapes=[pltpu.VMEM((A_HEADS, LANES, 2 * LANES), F32), pltpu.VMEM((A_HEADS, 1, LANES), F32)],
        compiler_params=_params("parallel", "arbitrary"),
        name=name,
    )(p, g, b_if, c0, m0, mh_norm_w, sg_norm_w, sg_w_tril, sg_b_t)


def _pack_state(C, n, m):
    b = C.shape[0]
    cn = jnp.concatenate([C, jnp.broadcast_to(n[..., None], C.shape[:3] + (LANES,))], axis=-1)
    z = jnp.zeros_like(cn)
    cn = cn.reshape(b, A_HEADS // 2, 2, A_DK, 2 * LANES)
    z = z.reshape(cn.shape)
    even = jnp.concatenate([cn[:, :, 0], z[:, :, 0]], axis=-2)
    odd = jnp.concatenate([z[:, :, 1], cn[:, :, 1]], axis=-2)
    c0 = jnp.stack([even, odd], axis=2).reshape(b, A_HEADS, 2 * A_DK, 2 * LANES)
    return c0, jnp.broadcast_to(m[:, :, None, None], (b, A_HEADS, 1, LANES))


def _unpack_state(cfin, mfin):
    b = cfin.shape[0]
    c5 = cfin.reshape(b, A_HEADS // 2, 2, 2, A_DK, 2 * LANES)
    cn = jnp.stack([c5[:, :, 0, 0], c5[:, :, 1, 1]], axis=2).reshape(b, A_HEADS, A_DK, 2 * LANES)
    return cn[..., :A_DV], cn[..., A_DV], mfin[:, :, 0, 0]


def _strict_upper(tk):
    r = lax.broadcasted_iota(jnp.int32, (tk, tk), 0)
    c = lax.broadcasted_iota(jnp.int32, (tk, tk), 1)
    return jnp.where(r > c, 1.0, 0.0).astype(BF16)


def _attn_kernel(bias_ref, q_ref, k_ref, v_ref, o_ref, acc_sc, z0_sc, z1_sc, a0_sc, a1_sc, *, tq):
    tk = PROMPT_TK
    nd = tq // tk
    nsub = PROMPT_BLOCKS_PER_TRIP
    span = nsub * tk
    assert nd % nsub == 0
    h = pl.program_id(1)
    qi = pl.program_id(2)
    ntrip = qi * (nd // nsub)
    bias2 = bias_ref[h] * LOG2E
    nsu = -_strict_upper(tk)

    def trip_start(t):
        return pl.multiple_of(jnp.maximum(qi * tq - (t + 1) * span, 0), tk)

    def weights(z2, ncarry, mask):
        e = jnp.exp2(jnp.minimum(z2, EXP2_MAX))
        sp = jnp.log(1.0 + e)
        if mask is not None:
            sp = jnp.where(mask, sp, 0.0)
        n = z2.shape[1] // tk
        parts = [None] * n
        for i in reversed(range(n)):
            sp_i = sp[:, i * tk:(i + 1) * tk]
            nrest = _mm(sp_i.astype(BF16), nsu)
            parts[i] = e[:, i * tk:(i + 1) * tk] * jnp.exp(nrest + ncarry - sp_i)
            ncarry = ncarry + nrest[:, 0:1] - sp_i[:, 0:1]
        a = parts[0] if n == 1 else jnp.concatenate(parts, axis=1)
        if mask is not None:
            a = jnp.where(mask, a, 0.0)
        return a.astype(BF16), ncarry

    z0_sc[...] = _nt(q_ref[...], k_ref[pl.ds(trip_start(0), span), :])
    a1_sc[...] = jnp.zeros_like(a1_sc)

    row = lax.broadcasted_iota(jnp.int32, (tk, tk), 0)
    col = lax.broadcasted_iota(jnp.int32, (tk, tk), 1)
    ncarries = []
    for rb in range(nd):
        q_rows = q_ref[rb * tk:(rb + 1) * tk, :]
        ncarry = jnp.zeros((tk, 1), F32)
        acc = jnp.zeros((tk, C_HD), F32)
        for sub in reversed(range(rb + 1)):
            start = pl.multiple_of(qi * tq + sub * tk, tk)
            z2 = _nt(q_rows, k_ref[pl.ds(start, tk), :]) + bias2
            a, ncarry = weights(z2, ncarry, col < row if sub == rb else None)
            acc = acc + _mm(a, v_ref[pl.ds(start, tk), :])
        acc_sc[rb * tk:(rb + 1) * tk, :] = acc
        ncarries.append(ncarry)

    def stage(t, z_cur, z_nxt, a_prev, a_cur, ncarry):
        acc_sc[...] += _mm(a_prev[...], v_ref[pl.ds(trip_start(t - 1), span), :])
        z_nxt[...] = _nt(q_ref[...], k_ref[pl.ds(trip_start(t + 1), span), :])
        a, ncarry = weights(z_cur[...] + bias2, ncarry, None)
        a_cur[...] = a
        return ncarry

    def two_trips(i, ncarry):
        ncarry = stage(2 * i, z0_sc, z1_sc, a1_sc, a0_sc, ncarry)
        return stage(2 * i + 1, z1_sc, z0_sc, a0_sc, a1_sc, ncarry)

    ncarry = lax.fori_loop(0, ntrip // 2, two_trips, jnp.concatenate(ncarries, axis=0))

    @pl.when(ntrip % 2 == 1)
    def _():
        stage(ntrip - 1, z0_sc, z1_sc, a1_sc, a0_sc, ncarry)
        acc_sc[...] += _mm(a0_sc[...], v_ref[pl.ds(trip_start(ntrip - 1), span), :])

    @pl.when(ntrip % 2 == 0)
    def _():
        acc_sc[...] += _mm(a1_sc[...], v_ref[pl.ds(trip_start(ntrip - 1), span), :])

    o_ref[...] = acc_sc[...].astype(o_ref.dtype)


def _attn(q, k, v, bias, n_seq, tq):
    n = q.shape[0]
    t = n // n_seq
    nq = t // tq
    span = PROMPT_BLOCKS_PER_TRIP * PROMPT_TK
    assert tq % span == 0
    return pl.pallas_call(
        functools.partial(_attn_kernel, tq=tq),
        grid=(n_seq, C_HEADS, nq),
        in_specs=[pl.BlockSpec(memory_space=pltpu.SMEM),
                  pl.BlockSpec((tq, C_HD), lambda b, h, i: (b * nq + i, h)),
                  pl.BlockSpec((t, C_HD), lambda b, h, i: (b, h)),
                  pl.BlockSpec((t, C_HD), lambda b, h, i: (b, h))],
        out_specs=pl.BlockSpec((tq, C_HD), lambda b, h, i: (b * nq + i, h)),
        out_shape=jax.ShapeDtypeStruct((n, C_WIDTH), BF16),
        scratch_shapes=[pltpu.VMEM((tq, C_HD), F32)]
                       + [pltpu.VMEM((tq, span), F32)] * 2 + [pltpu.VMEM((tq, span), BF16)] * 2,
        compiler_params=_params("parallel", "parallel", "arbitrary"),
        name="sb_attention",
    )(bias, q, k, v)


def _decode_kernel(pt_ref, q_ref, kn_ref, vn_ref, *refs, n_new, n_pg):
    del pt_ref
    kc_refs, vc_refs = refs[:n_pg], refs[n_pg:2 * n_pg]
    bias_ref, o_ref, acc_sc, carry_sc = refs[2 * n_pg:]
    tk = ATT_TK
    R = SAMPLE_ROWS
    HR = C_HEADS * R
    j = pl.program_id(1)
    su = _strict_upper(tk)

    def weights(z, mask, carry_in):
        sp = _softplus(z)
        if mask is not None:
            sp = jnp.where(mask, sp, 0.0)
        hi = sp.astype(BF16)
        lo = (sp - hi.astype(F32)).astype(BF16)
        rest = _mm(hi, su) + _mm(lo, su)
        tot = rest[:, 0:1] + sp[:, 0:1]
        carries = [carry_in]
        for p in range(z.shape[0] // HR):
            carries.append(carries[-1] + tot[p * HR:(p + 1) * HR])
        a = jnp.exp(z - sp - rest - jnp.concatenate(carries[:-1], axis=0))
        if mask is not None:
            a = jnp.where(mask, a, 0.0)
        return a, carries[-1]

    @pl.when(j == 0)
    def _():
        pad = jnp.zeros((tk - R, C_HD), F32)
        zs, vals = [], []
        for h in range(C_HEADS):
            sl = slice(h * C_HD, (h + 1) * C_HD)
            kh = jnp.concatenate([kn_ref[:, sl], pad], axis=0).astype(BF16)
            vals.append(jnp.concatenate([vn_ref[:, sl], pad], axis=0).astype(BF16))
            zs.append(_nt(q_ref[:, sl].astype(BF16), kh))
        z = jnp.concatenate(zs, axis=0) * SB_SCALE + bias_ref[...]
        row = lax.broadcasted_iota(jnp.int32, z.shape, 0)
        col = lax.broadcasted_iota(jnp.int32, z.shape, 1)
        a, carry = weights(z, (col < row % R) & (col < n_new), jnp.zeros((HR, 1), F32))
        for h in range(C_HEADS):
            acc_sc[h * R:(h + 1) * R, :] = _mm(a[h * R:(h + 1) * R].astype(BF16), vals[h])
        carry_sc[...] = jnp.broadcast_to(carry, carry_sc.shape)

    zs = [[None] * C_HEADS for _ in range(n_pg)]
    vals = []
    for h in range(C_HEADS):
        sl = slice(h * C_HD, (h + 1) * C_HD)
        kh = jnp.concatenate([r[pl.ds(h, tk, stride=C_HEADS), :].astype(BF16) for r in kc_refs], axis=0)
        vals.append(jnp.concatenate([r[pl.ds(h, tk, stride=C_HEADS), :].astype(BF16) for r in vc_refs], axis=0))
        zh = _nt(q_ref[:, sl].astype(BF16), kh)
        for p in range(n_pg):
            zs[p][h] = zh[:, p * tk:(p + 1) * tk]
    z = jnp.concatenate([zs[p][h] for p in range(n_pg) for h in range(C_HEADS)], axis=0)
    z = z * SB_SCALE + jnp.concatenate([bias_ref[...]] * n_pg, axis=0)
    a, carry = weights(z, None, carry_sc[:, 0:1])
    for h in range(C_HEADS):
        ah = jnp.concatenate([a[(p * C_HEADS + h) * R:(p * C_HEADS + h + 1) * R] for p in range(n_pg)], axis=1)
        acc_sc[h * R:(h + 1) * R, :] += _mm(ah.astype(BF16), vals[h])
    carry_sc[...] = jnp.broadcast_to(carry, carry_sc.shape)

    @pl.when(j == pl.num_programs(1) - 1)
    def _():
        for h in range(C_HEADS):
            o_ref[:, h * C_HD:(h + 1) * C_HD] = acc_sc[h * R:(h + 1) * R, :]


def _decode_attn(q, k_new, v_new, cache_k, cache_v, page_table, bias_rows, layer, n_new):
    n_seq, n_pages = page_table.shape
    n_phys, n_att, page, heads, hd = cache_k.shape
    n_pg = DECODE_PAGES
    assert n_pages % n_pg == 0 and page == ATT_TK
    ck = cache_k.reshape(n_phys, n_att, page * heads, hd)
    cv = cache_v.reshape(n_phys, n_att, page * heads, hd)
    R = SAMPLE_ROWS
    row_map = lambda b, j, pt: (b, 0)

    def page_spec(p):
        return pl.BlockSpec((None, None, page * heads, hd),
                            lambda b, j, pt: (pt[b, n_pages - 1 - (j * n_pg + p)], layer, 0, 0))

    grid_spec = pltpu.PrefetchScalarGridSpec(
        num_scalar_prefetch=1,
        grid=(n_seq, n_pages // n_pg),
        in_specs=[pl.BlockSpec((R, C_WIDTH), row_map)] * 3
                 + [page_spec(p) for p in range(n_pg)] * 2
                 + [pl.BlockSpec((C_HEADS * R, LANES), lambda b, j, pt: (0, 0))],
        out_specs=pl.BlockSpec((R, C_WIDTH), row_map),
        scratch_shapes=[pltpu.VMEM((C_HEADS * R, C_HD), F32), pltpu.VMEM((C_HEADS * R, LANES), F32)],
    )
    return pl.pallas_call(
        functools.partial(_decode_kernel, n_new=n_new, n_pg=n_pg),
        grid_spec=grid_spec,
        out_shape=jax.ShapeDtypeStruct((n_seq * R, C_WIDTH), F32),
        compiler_params=_params("parallel", "arbitrary"),
        name="sb_decode",
    )(page_table, q, k_new, v_new, *([ck] * n_pg), *([cv] * n_pg), bias_rows)


def _trunk(x, mods, W, n_seq, rows_per_seq, tm, state, valid_len, past):
    n = x.shape[0]
    depth = W['w_up'].shape[0]
    n_att = W['w_qkv_c'].shape[0]
    block_rows = min(rows_per_seq, MIX_CHUNK * MIX_CHUNKS_PER_STEP)
    rec, rows, ks, vs = [], [], [], []
    if past is None:
        ks = vs = n_att
    for l in range(depth):
        sh1, sc1, g1, sh2, sc2, g2 = mods[l]
        if l % 2 == 0:
            e = l // 2
            p, gates = _proj(x, W['norm_w'][l, 0], sh1, sc1, W['w_in_ab'], e,
                             [(0, AB_MAIN, F32, None, None), (AB_MAIN, AB_MAIN + LANES, F32, None, None)],
                             tm, rows_per_seq, "proj_even")
            c0, m0 = _pack_state(*state(e))
            mixed, r, cfin, mfin = _mix(p, gates, W['b_if'][e], c0, m0, W['mh_norm_w'][e], W['sg_norm_w'][e],
                                        W['sg_w'][e], W['sg_b'][e], n_seq, block_rows, valid_len, "mix_even")
            rec.append(_unpack_state(cfin, mfin))
            rows.append(r)
        else:
            o = l // 2
            kcols, vcols = (C_WIDTH, 2 * C_WIDTH), (2 * C_WIDTH, 3 * C_WIDTH)
            if past is None:
                q, ks, vs, kb, vb = _proj(x, W['norm_w'][l, 0], sh1, sc1, W['w_qkv_c'], o,
                                          [(0, C_WIDTH, BF16, SB_SCALE * LOG2E, None),
                                           kcols + (F32, None, ks), vcols + (F32, None, vs),
                                           kcols + (BF16, None, None), vcols + (BF16, None, None)],
                                          tm, rows_per_seq, "proj_odd")
                mixed = _attn(q, kb, vb, W['b_sb'][o], n_seq, min(PROMPT_TQ, rows_per_seq))
            else:
                q, k, v = _proj(x, W['norm_w'][l, 0], sh1, sc1, W['w_qkv_c'], o,
                                [(0, C_WIDTH, F32, None, None), kcols + (F32, None, None),
                                 vcols + (F32, None, None)], tm, rows_per_seq, "proj_odd_s")
                bias_rows = jnp.broadcast_to(jnp.repeat(W['b_sb'][o], SAMPLE_ROWS)[:, None],
                                             (C_HEADS * SAMPLE_ROWS, LANES))
                mixed = _decode_attn(q, k, v, past[0], past[1], past[2], bias_rows, o, valid_len)
                ks.append(k)
                vs.append(v)
        x = _tail(mixed, W['w_out'], l, x, g1, W['norm_w'][l, 1], sh2, sc2, g2, W['w_up'], W['w_down'], l,
                  W['final_norm_w'], l == depth - 1, min(MLP_TM, n), MLP_TF, rows_per_seq, "tail")
    return x, rec, rows, ks, vs


def kernel(x_prompt, x_sample, state_mlstm_C, state_mlstm_n, state_mlstm_m, cache_k, cache_v, page_table, c_prompt, c_sample, w_ada, b_ada, norm_w, final_norm_w, w_in_ab, b_if, mh_norm_w, sg_norm_w, sg_w, sg_b, w_out_ab, w_qkv_c, b_sb, w_out_c, w_up, w_down):
    bp, t_p, d = x_prompt.shape
    bs, t_s, _ = x_sample.shape
    depth = w_ada.shape[0]
    n_rec = w_in_ab.shape[0]
    R = SAMPLE_ROWS
    assert t_s <= R and t_p % MIX_CHUNK == 0

    w_in = jnp.concatenate([w_in_ab[:, :, :AB_MAIN], w_in_ab[:, :, AB_MAIN:],
                            jnp.zeros((n_rec, d, LANES - 2 * A_HEADS), F32)], axis=-1).astype(BF16)
    causal = jnp.tril(jnp.ones((MIX_CHUNK, MIX_CHUNK), bool))
    W = dict(
        norm_w=norm_w, final_norm_w=final_norm_w, w_in_ab=w_in,
        b_if=jnp.concatenate([b_if, jnp.zeros((n_rec, LANES - 2 * A_HEADS), F32)], axis=-1).reshape(n_rec, 1, LANES),
        mh_norm_w=mh_norm_w.reshape(n_rec, 1, A_WIDTH), sg_norm_w=sg_norm_w.reshape(n_rec, 1, B_WIDTH),
        sg_w=jnp.where(causal, sg_w, 0.0).astype(BF16),
        sg_b=jnp.broadcast_to(sg_b[:, :, :MIX_CHUNK, None], (n_rec, B_GROUPS, MIX_CHUNK, LANES)),
        w_out=jnp.stack([(w_out_ab if l % 2 == 0 else w_out_c)[l // 2] for l in range(depth)]).astype(BF16),
        w_qkv_c=w_qkv_c.astype(BF16), b_sb=b_sb, w_up=w_up, w_down=w_down)

    n_c = bp + bs
    n_c_pad = -(-n_c // 8) * 8
    c_all = jnp.concatenate([c_prompt, c_sample, jnp.zeros((n_c_pad - n_c, d), F32)], axis=0)
    mod = _ada(c_all, w_ada, b_ada).reshape(depth, n_c_pad, N_MOD, d)
    mod_p = jnp.transpose(mod[:, :bp], (0, 2, 1, 3)).reshape(depth * N_MOD * bp, 1, d)
    mods_p = [[(mod_p, (l * N_MOD + i) * bp) for i in range(N_MOD)] for l in range(depth)]
    mod_s = jnp.transpose(jnp.repeat(mod[:, bp:bp + bs], R, axis=1), (0, 2, 1, 3))
    mod_s = mod_s.reshape(depth * N_MOD, bs * R, d)
    mods_s = [[(mod_s, l * N_MOD + i) for i in range(N_MOD)] for l in range(depth)]

    zero_state = (jnp.zeros((bp, A_HEADS, A_DK, A_DV), F32), jnp.zeros((bp, A_HEADS, A_DK), F32),
                  jnp.zeros((bp, A_HEADS), F32))
    yp, rec_p, _, ks_p, vs_p = _trunk(x_prompt.reshape(bp * t_p, d), mods_p, W, bp, t_p, 512,
                                      lambda e: zero_state, MIX_CHUNK, None)

    xs = jnp.pad(x_sample, ((0, 0), (0, R - t_s), (0, 0))).reshape(bs * R, d)
    ys, rec_s, rows_s, ks_s, vs_s = _trunk(
        xs, mods_s, W, bs, R, bs * R,
        lambda e: (state_mlstm_C[e], state_mlstm_n[e], state_mlstm_m[e]), t_s, (cache_k, cache_v, page_table))

    def unpad(a, tail):
        return a.reshape((bs, R) + tail)[:, :t_s]

    return (yp.reshape(bp, t_p, d), unpad(ys, (d,)),
            jnp.stack([r[0] for r in rec_p]), jnp.stack([r[1] for r in rec_p]), jnp.stack([r[2] for r in rec_p]),
            jnp.stack([r[0] for r in rec_s]), jnp.stack([r[1] for r in rec_s]), jnp.stack([r[2] for r in rec_s]),
            jnp.stack([unpad(r, (B_WIDTH,)) for r in rows_s]),
            ks_p.reshape(-1, bp, t_p, C_HEADS, C_HD), vs_p.reshape(-1, bp, t_p, C_HEADS, C_HD),
            jnp.stack([unpad(k, (C_HEADS, C_HD)) for k in ks_s]),
            jnp.stack([unpad(v, (C_HEADS, C_HD)) for v in vs_s]))
```

```python
import functools

import jax
import jax.numpy as jnp
from jax import lax
from jax.experimental import pallas as pl
from jax.experimental.pallas import tpu as pltpu

F32 = jnp.float32
BF16 = jnp.bfloat16

D_MODEL = 1024
A_HEADS = 4
A_DV = 128
A_DK = 64
A_WIDTH = A_HEADS * A_DV
A_QK_W = A_HEADS * A_DK
B_GROUPS = 4
B_WIDTH = 512
B_GROUP_DIM = 128
C_HEADS = 8
C_HD = 128
C_WIDTH = C_HEADS * C_HD
SB_SCALE = C_HD ** -0.5
D_FF = 4 * D_MODEL
N_MOD = 6
EPS = 1e-6
AB_MAIN = 2 * A_QK_W + 2 * A_WIDTH + 2 * B_WIDTH
MIX_CHUNK = 128
MIX_CHUNKS_PER_STEP = 4
ATT_TK = 128
PROMPT_TK = 256
PROMPT_TQ = 512
PROMPT_BLOCKS_PER_TRIP = 1
DECODE_PAGES = 16
MLP_TM = 1024
MLP_TF = 512
LOG2E = 1.4426950408889634
EXP2_MAX = 126.0
SAMPLE_ROWS = 8
NEG = -1e30
LANES = 128
VMEM_LIMIT_V7X = 56 * 1024 * 1024


def _mm(a, b):
    return jnp.dot(a, b, preferred_element_type=F32)


def _nt(a, b):
    return lax.dot_general(a, b, (((1,), (1,)), ((), ())), preferred_element_type=F32)


def _tn(a, b):
    return lax.dot_general(a, b, (((0,), (0,)), ((), ())), preferred_element_type=F32)


def _softplus(z):
    return jnp.maximum(z, 0.0) + jnp.log1p(jnp.exp(-jnp.abs(z)))


def _rms(x, w):
    return x * lax.rsqrt(jnp.mean(x * x, axis=-1, keepdims=True) + EPS) * w


def _rms_mod(x, w, shift, scale):
    return _rms(x, w) * (1.0 + scale) + shift


def _split3(x):
    p1 = x.astype(BF16)
    r1 = x - p1.astype(F32)
    p2 = r1.astype(BF16)
    p3 = (r1 - p2.astype(F32)).astype(BF16)
    return p1, p2, p3


def _params(*sem):
    return pltpu.CompilerParams(dimension_semantics=sem, vmem_limit_bytes=VMEM_LIMIT_V7X)


def _mod_spec(mod, tm, rows_per_group):
    arr, base = mod
    g, r, d = arr.shape
    assert r == 1 or r == tm
    if r == 1:
        bpg = rows_per_group // tm
        return pl.BlockSpec((None, r, d), lambda i, *_: (base + i // bpg, 0, 0))
    return pl.BlockSpec((None, r, d), lambda i, *_: (base, 0, 0))


def _ada_kernel(c_ref, w_ref, b_ref, o_ref):
    c = c_ref[...]
    s = (c * jax.nn.sigmoid(c)).astype(BF16)
    o_ref[...] = _mm(s, w_ref[...].astype(BF16)) + b_ref[...]


def _ada(c_all, w_ada, b_ada):
    depth, d, n = w_ada.shape
    rows = c_all.shape[0]
    tn = 1536
    return pl.pallas_call(
        _ada_kernel,
        grid=(depth, n // tn),
        in_specs=[pl.BlockSpec((rows, d), lambda l, j: (0, 0)),
                  pl.BlockSpec((None, d, tn), lambda l, j: (l, 0, j)),
                  pl.BlockSpec((None, 1, tn), lambda l, j: (l, 0, j))],
        out_specs=pl.BlockSpec((None, rows, tn), lambda l, j: (l, 0, j)),
        out_shape=jax.ShapeDtypeStruct((depth, rows, n), F32),
        compiler_params=_params("parallel", "parallel"),
        name="ada_mod",
    )(c_all, w_ada, b_ada.reshape(depth, 1, n))


def _proj_kernel(x_ref, nw_ref, sh_ref, sc_ref, w_ref, *refs, outs):
    o_refs = refs[len(refs) - len(outs):]
    h = _rms_mod(x_ref[...], nw_ref[...], sh_ref[...], sc_ref[...]).astype(BF16)
    products = {}
    for o_ref, (a, b, _, mult, stacked) in zip(o_refs, outs):
        if (a, b) not in products:
            products[(a, b)] = _mm(h, w_ref[:, a:b])
        y = products[(a, b)]
        if mult is not None:
            y = y * mult
        if stacked == "update":
            o_ref[...] = y.reshape(o_ref.shape).astype(o_ref.dtype)
        elif stacked is not None:
            slot, n_slots = stacked
            for s in range(n_slots):
                val = y.reshape(o_ref.shape[1:]) if s == slot else jnp.zeros(o_ref.shape[1:], F32)
                o_ref[s] = val.astype(o_ref.dtype)
        else:
            o_ref[...] = y.astype(o_ref.dtype)


def _proj(x, nw, shift, scale, w_all, layer, outs, tm, rows_per_group, name):
    n, d = x.shape
    out_specs, out_shape, stacked_in, aliases, modes = [], [], [], {}, []
    for idx, (a, b, dt, _, stacked) in enumerate(outs):
        heads = (b - a) // LANES
        if stacked is None:
            out_specs.append(pl.BlockSpec((tm, b - a), lambda i: (i, 0)))
            out_shape.append(jax.ShapeDtypeStruct((n, b - a), dt))
            modes.append(None)
        elif isinstance(stacked, int):
            out_specs.append(pl.BlockSpec((stacked, tm, heads, LANES), lambda i: (0, i, 0, 0)))
            out_shape.append(jax.ShapeDtypeStruct((stacked, n, heads, LANES), dt))
            modes.append((layer, stacked))
        else:
            out_specs.append(pl.BlockSpec((None, tm) + stacked.shape[2:], lambda i: (layer, i, 0, 0)))
            out_shape.append(jax.ShapeDtypeStruct(stacked.shape, stacked.dtype))
            aliases[5 + len(stacked_in)] = idx
            stacked_in.append(stacked)
            modes.append("update")
    return pl.pallas_call(
        functools.partial(_proj_kernel, outs=tuple(o[:4] + (m,) for o, m in zip(outs, modes))),
        grid=(n // tm,),
        in_specs=[pl.BlockSpec((tm, d), lambda i: (i, 0)),
                  pl.BlockSpec((1, d), lambda i: (0, 0)),
                  _mod_spec(shift, tm, rows_per_group),
                  _mod_spec(scale, tm, rows_per_group),
                  pl.BlockSpec((None,) + w_all.shape[1:], lambda i: (layer, 0, 0))]
                 + [pl.BlockSpec(memory_space=pl.ANY)] * len(stacked_in),
        out_specs=out_specs,
        out_shape=out_shape,
        input_output_aliases=aliases,
        compiler_params=_params("parallel"),
        name=name,
    )(x, nw.reshape(1, d), shift[0], scale[0], w_all, *stacked_in)


def _tail_kernel(a_ref, wo_ref, x_ref, g1_ref, nw_ref, sh_ref, sc_ref, g2_ref, wu_ref, wd_ref, fnw_ref,
                 o_ref, h_sc, acc_sc, *, final):
    f = pl.program_id(1)

    @pl.when(f == 0)
    def _():
        x1 = x_ref[...] + g1_ref[...] * _mm(a_ref[...].astype(BF16), wo_ref[...])
        o_ref[...] = x1
        h_sc[...] = _rms_mod(x1, nw_ref[...], sh_ref[...], sc_ref[...]).astype(BF16)
        acc_sc[...] = jnp.zeros_like(acc_sc)

    up = _mm(h_sc[...], wu_ref[...].astype(BF16))
    act = jnp.square(jnp.maximum(up, 0.0)).astype(BF16)
    acc_sc[...] += _mm(act, wd_ref[...].astype(BF16))

    @pl.when(f == pl.num_programs(1) - 1)
    def _():
        y = o_ref[...] + g2_ref[...] * acc_sc[...]
        if final:
            y = _rms(y, fnw_ref[...])
        o_ref[...] = y


def _tail(a, w_out_all, mix_layer, x, gate1, nw, shift, scale, gate2, w_up_all, w_down_all, layer, fnw, final,
          tm, tf, rows_per_group, name):
    n, d = x.shape
    dff = w_up_all.shape[2]
    rows = lambda i, f: (i, 0)
    const = lambda i, f: (0, 0)
    return pl.pallas_call(
        functools.partial(_tail_kernel, final=final),
        grid=(n // tm, dff // tf),
        in_specs=[pl.BlockSpec((tm, a.shape[1]), rows),
                  pl.BlockSpec((None,) + w_out_all.shape[1:], lambda i, f: (mix_layer, 0, 0)),
                  pl.BlockSpec((tm, d), rows),
                  _mod_spec(gate1, tm, rows_per_group),
                  pl.BlockSpec((1, d), const),
                  _mod_spec(shift, tm, rows_per_group),
                  _mod_spec(scale, tm, rows_per_group),
                  _mod_spec(gate2, tm, rows_per_group),
                  pl.BlockSpec((None, d, tf), lambda i, f: (layer, 0, f)),
                  pl.BlockSpec((None, tf, d), lambda i, f: (layer, f, 0)),
                  pl.BlockSpec((1, d), const)],
        out_specs=pl.BlockSpec((tm, d), rows),
        out_shape=jax.ShapeDtypeStruct((n, d), F32),
        scratch_shapes=[pltpu.VMEM((tm, d), BF16), pltpu.VMEM((tm, d), F32)],
        compiler_params=_params("parallel", "arbitrary"),
        name=name,
    )(a, w_out_all, x, gate1[0], nw.reshape(1, d), shift[0], scale[0], gate2[0], w_up_all, w_down_all,
      fnw.reshape(1, d))


def _mix_kernel(p_ref, g_ref, bif_ref, c0_ref, m0_ref, mhw_ref, sgnw_ref, sgw_ref, sgb_ref,
                hz_ref, rows_ref, cfin_ref, mfin_ref, c_sc, m_sc, *, block_rows, valid_len):
    L = MIX_CHUNK
    c = pl.program_id(1)

    @pl.when(c == 0)
    def _():
        c_sc[...] = c0_ref[...]
        m_sc[...] = m0_ref[...]

    rowi = lax.broadcasted_iota(jnp.int32, (L, LANES), 0)
    lanei = lax.broadcasted_iota(jnp.int32, (L, LANES), 1)
    is_i = lanei < A_HEADS
    is_f = (lanei >= A_HEADS) & (lanei < 2 * A_HEADS)
    sq_r = lax.broadcasted_iota(jnp.int32, (L, L), 0)
    sq_c = lax.broadcasted_iota(jnp.int32, (L, L), 1)
    causal = sq_c <= sq_r
    tril = jnp.where(causal, 1.0, 0.0).astype(BF16)
    eye = jnp.where(sq_c == sq_r, 1.0, 0.0).astype(BF16)
    ones = jnp.ones((L, LANES), F32)
    sel_r = lax.broadcasted_iota(jnp.int32, (LANES, 2 * LANES), 0)
    sel_c = lax.broadcasted_iota(jnp.int32, (LANES, 2 * LANES), 1)
    sel = [jnp.where(((sel_r == h) & (sel_c < LANES)) | ((sel_r == A_HEADS + h) & (sel_c >= LANES)), 1.0,
                     0.0).astype(BF16) for h in range(A_HEADS)]

    def chunk(p, g):
        gb = g + bif_ref[...]
        x8 = jnp.where(is_i, gb, jnp.where(is_f, -_softplus(-gb), 0.0))
        if valid_len < L:
            x8 = jnp.where(rowi < valid_len, x8, jnp.where(is_i, NEG, 0.0))

        parts = _split3(x8)
        bc = _mm(tril, parts[0]) + _mm(tril, parts[1]) + _mm(tril, parts[2])
        yparts = _split3(jnp.where(is_i, x8, bc))
        yparts_t = [_nt(eye, q) for q in yparts]
        y_t = yparts_t[0] + yparts_t[1] + yparts_t[2]

        has = []
        for h in range(A_HEADS):
            hp, hl = h // 2, h % 2
            headmask = (lanei >= A_DK * hl) & (lanei < A_DK * (hl + 1))
            q2 = p[:, hp * LANES:(hp + 1) * LANES]
            k2 = p[:, A_QK_W + hp * LANES:A_QK_W + (hp + 1) * LANES]
            v = p[:, 2 * A_QK_W + h * A_DV:2 * A_QK_W + (h + 1) * A_DV]
            qh = jnp.where(headmask, q2, 0.0).astype(BF16)
            kh = jnp.where(headmask, k2 * (A_DK ** -0.5), 0.0)
            rep = _mm(yparts[0], sel[h]) + _mm(yparts[1], sel[h]) + _mm(yparts[2], sel[h])
            ig_rep, b_rep = rep[:, :LANES], rep[:, LANES:]
            m_prev = m_sc[h]

            gk = jnp.where(causal, y_t[h:h + 1, :] - y_t[A_HEADS + h:A_HEADS + h + 1, :], NEG)
            mx = jnp.maximum(m_prev, jnp.max(gk, axis=1, keepdims=True))
            m_t = b_rep + mx
            w_st = jnp.exp(m_prev - mx)
            pmat = _nt(qh, kh.astype(BF16)) * jnp.exp(gk - mx)
            v_ext = jnp.concatenate([v, ones], axis=1).astype(BF16)
            c_old = c_sc[h]
            num_ext = _mm(pmat.astype(BF16), v_ext) + jnp.concatenate([w_st, w_st], axis=1) * _mm(
                qh, c_old.astype(BF16))
            hh = num_ext[:, :A_DV] / jnp.maximum(jnp.abs(num_ext[:, A_DV:]), jnp.exp(-m_t))

            m_new = m_t[L - 1:L, :]
            b_end = b_rep[L - 1:L, :]
            w_k = jnp.exp(b_end - b_rep + ig_rep - m_new)
            decay = jnp.exp(b_end + m_prev - m_new)
            c_sc[h] = jnp.concatenate([decay, decay], axis=1) * c_old + _tn((kh * w_k).astype(BF16), v_ext)
            m_sc[h] = m_new

            o = p[:, 2 * A_QK_W + A_WIDTH + h * A_DV:2 * A_QK_W + A_WIDTH + (h + 1) * A_DV]
            has.append(_rms(hh, mhw_ref[:, h * A_DV:(h + 1) * A_DV]) * jax.nn.sigmoid(o))

        u0 = 2 * A_QK_W + 2 * A_WIDTH
        zbs, vsn = [], []
        for gi in range(B_GROUPS):
            u = jax.nn.gelu(p[:, u0 + gi * B_GROUP_DIM:u0 + (gi + 1) * B_GROUP_DIM])
            vs = jax.nn.gelu(p[:, u0 + B_WIDTH + gi * B_GROUP_DIM:u0 + B_WIDTH + (gi + 1) * B_GROUP_DIM])
            vs = _rms(vs, sgnw_ref[:, gi * B_GROUP_DIM:(gi + 1) * B_GROUP_DIM])
            s = _mm(sgw_ref[gi], vs.astype(BF16)) + sgb_ref[gi]
            zbs.append(u * s)
            vsn.append(vs)
        return jnp.concatenate(has + zbs, axis=1), jnp.concatenate(vsn, axis=1)

    if block_rows < L:
        pad = lambda a: jnp.concatenate([a, jnp.zeros((L - block_rows, a.shape[1]), F32)], axis=0)
        hz, vsn = chunk(pad(p_ref[...]), pad(g_ref[...]))
        hz_ref[...] = hz[:block_rows].astype(hz_ref.dtype)
        rows_ref[...] = vsn[:block_rows]
    else:
        for ci in range(block_rows // L):
            sl = slice(ci * L, (ci + 1) * L)
            hz, vsn = chunk(p_ref[sl, :], g_ref[sl, :])
            hz_ref[sl, :] = hz.astype(hz_ref.dtype)
            rows_ref[sl, :] = vsn

    @pl.when(c == pl.num_programs(1) - 1)
    def _():
        cfin_ref[...] = c_sc[...]
        mfin_ref[...] = m_sc[...]


def _mix(p, g, b_if, c0, m0, mh_norm_w, sg_norm_w, sg_w_tril, sg_b_t, n_seq, block_rows, valid_len, name):
    n = p.shape[0]
    nc = n // (n_seq * block_rows)
    row_map = lambda b, c: (b * nc + c, 0)
    const2 = lambda b, c: (0, 0)
    return pl.pallas_call(
        functools.partial(_mix_kernel, block_rows=block_rows, valid_len=valid_len),
        grid=(n_seq, nc),
        in_specs=[pl.BlockSpec((block_rows, AB_MAIN), row_map),
                  pl.BlockSpec((block_rows, LANES), row_map),
                  pl.BlockSpec((1, LANES), const2),
                  pl.BlockSpec((None, A_HEADS, LANES, 2 * LANES), lambda b, c: (b, 0, 0, 0)),
                  pl.BlockSpec((None, A_HEADS, 1, LANES), lambda b, c: (b, 0, 0, 0)),
                  pl.BlockSpec((1, A_WIDTH), const2),
                  pl.BlockSpec((1, B_WIDTH), const2),
                  pl.BlockSpec((B_GROUPS, MIX_CHUNK, MIX_CHUNK), lambda b, c: (0, 0, 0)),
                  pl.BlockSpec((B_GROUPS, MIX_CHUNK, LANES), lambda b, c: (0, 0, 0))],
        out_specs=[pl.BlockSpec((block_rows, A_WIDTH + B_WIDTH), row_map),
                   pl.BlockSpec((block_rows, B_WIDTH), row_map),
                   pl.BlockSpec((None, A_HEADS, LANES, 2 * LANES), lambda b, c: (b, 0, 0, 0)),
                   pl.BlockSpec((None, A_HEADS, 1, LANES), lambda b, c: (b, 0, 0, 0))],
        out_shape=[jax.ShapeDtypeStruct((n, A_WIDTH + B_WIDTH), BF16),
                   jax.ShapeDtypeStruct((n, B_WIDTH), F32),
                   jax.ShapeDtypeStruct((n_seq, A_HEADS, LANES, 2 * LANES), F32),
                   jax.ShapeDtypeStruct((n_seq, A_HEADS, 1, LANES), F32)],
        scratch_shapes=[pltpu.VMEM((A_HEADS, LANES, 2 * LANES), F32), pltpu.VMEM((A_HEADS, 1, LANES), F32)],
        compiler_params=_params("parallel", "arbitrary"),
        name=name,
    )(p, g, b_if, c0, m0, mh_norm_w, sg_norm_w, sg_w_tril, sg_b_t)


def _pack_state(C, n, m):
    b = C.shape[0]
    cn = jnp.concatenate([C, jnp.broadcast_to(n[..., None], C.shape[:3] + (LANES,))], axis=-1)
    z = jnp.zeros_like(cn)
    cn = cn.reshape(b, A_HEADS // 2, 2, A_DK, 2 * LANES)
    z = z.reshape(cn.shape)
    even = jnp.concatenate([cn[:, :, 0], z[:, :, 0]], axis=-2)
    odd = jnp.concatenate([z[:, :, 1], cn[:, :, 1]], axis=-2)
    c0 = jnp.stack([even, odd], axis=2).reshape(b, A_HEADS, 2 * A_DK, 2 * LANES)
    return c0, jnp.broadcast_to(m[:, :, None, None], (b, A_HEADS, 1, LANES))


def _unpack_state(cfin, mfin):
    b = cfin.shape[0]
    c5 = cfin.reshape(b, A_HEADS // 2, 2, 2, A_DK, 2 * LANES)
    cn = jnp.stack([c5[:, :, 0, 0], c5[:, :, 1, 1]], axis=2).reshape(b, A_HEADS, A_DK, 2 * LANES)
    return cn[..., :A_DV], cn[..., A_DV], mfin[:, :, 0, 0]


def _strict_upper(tk):
    r = lax.broadcasted_iota(jnp.int32, (tk, tk), 0)
    c = lax.broadcasted_iota(jnp.int32, (tk, tk), 1)
    return jnp.where(r > c, 1.0, 0.0).astype(BF16)


def _attn_kernel(bias_ref, q_ref, k_ref, v_ref, o_ref, acc_sc, z0_sc, z1_sc, a0_sc, a1_sc, *, tq):
    tk = PROMPT_TK
    nd = tq // tk
    nsub = PROMPT_BLOCKS_PER_TRIP
    span = nsub * tk
    assert nd % nsub == 0
    h = pl.program_id(1)
    qi = pl.program_id(2)
    ntrip = qi * (nd // nsub)
    bias2 = bias_ref[h] * LOG2E
    nsu = -_strict_upper(tk)

    def trip_start(t):
        return pl.multiple_of(jnp.maximum(qi * tq - (t + 1) * span, 0), tk)

    def weights(z2, ncarry, mask):
        e = jnp.exp2(jnp.minimum(z2, EXP2_MAX))
        sp = jnp.log(1.0 + e)
        if mask is not None:
            sp = jnp.where(mask, sp, 0.0)
        n = z2.shape[1] // tk
        parts = [None] * n
        for i in reversed(range(n)):
            sp_i = sp[:, i * tk:(i + 1) * tk]
            nrest = _mm(sp_i.astype(BF16), nsu)
            parts[i] = e[:, i * tk:(i + 1) * tk] * jnp.exp(nrest + ncarry - sp_i)
            ncarry = ncarry + nrest[:, 0:1] - sp_i[:, 0:1]
        a = parts[0] if n == 1 else jnp.concatenate(parts, axis=1)
        if mask is not None:
            a = jnp.where(mask, a, 0.0)
        return a.astype(BF16), ncarry

    z0_sc[...] = _nt(q_ref[...], k_ref[pl.ds(trip_start(0), span), :])
    a1_sc[...] = jnp.zeros_like(a1_sc)

    row = lax.broadcasted_iota(jnp.int32, (tk, tk), 0)
    col = lax.broadcasted_iota(jnp.int32, (tk, tk), 1)
    ncarries = []
    for rb in range(nd):
        q_rows = q_ref[rb * tk:(rb + 1) * tk, :]
        ncarry = jnp.zeros((tk, 1), F32)
        acc = jnp.zeros((tk, C_HD), F32)
        for sub in reversed(range(rb + 1)):
            start = pl.multiple_of(qi * tq + sub * tk, tk)
            z2 = _nt(q_rows, k_ref[pl.ds(start, tk), :]) + bias2
            a, ncarry = weights(z2, ncarry, col < row if sub == rb else None)
            acc = acc + _mm(a, v_ref[pl.ds(start, tk), :])
        acc_sc[rb * tk:(rb + 1) * tk, :] = acc
        ncarries.append(ncarry)

    def stage(t, z_cur, z_nxt, a_prev, a_cur, ncarry):
        acc_sc[...] += _mm(a_prev[...], v_ref[pl.ds(trip_start(t - 1), span), :])
        z_nxt[...] = _nt(q_ref[...], k_ref[pl.ds(trip_start(t + 1), span), :])
        a, ncarry = weights(z_cur[...] + bias2, ncarry, None)
        a_cur[...] = a
        return ncarry

    def two_trips(i, ncarry):
        ncarry = stage(2 * i, z0_sc, z1_sc, a1_sc, a0_sc, ncarry)
        return stage(2 * i + 1, z1_sc, z0_sc, a0_sc, a1_sc, ncarry)

    ncarry = lax.fori_loop(0, ntrip // 2, two_trips, jnp.concatenate(ncarries, axis=0))

    @pl.when(ntrip % 2 == 1)
    def _():
        stage(ntrip - 1, z0_sc, z1_sc, a1_sc, a0_sc, ncarry)
        acc_sc[...] += _mm(a0_sc[...], v_ref[pl.ds(trip_start(ntrip - 1), span), :])

    @pl.when(ntrip % 2 == 0)
    def _():
        acc_sc[...] += _mm(a1_sc[...], v_ref[pl.ds(trip_start(ntrip - 1), span), :])

    o_ref[...] = acc_sc[...].astype(o_ref.dtype)


def _attn(q, k, v, bias, n_seq, tq):
    n = q.shape[0]
    t = n // n_seq
    nq = t // tq
    span = PROMPT_BLOCKS_PER_TRIP * PROMPT_TK
    assert tq % span == 0
    return pl.pallas_call(
        functools.partial(_attn_kernel, tq=tq),
        grid=(n_seq, C_HEADS, nq),
        in_specs=[pl.BlockSpec(memory_space=pltpu.SMEM),
                  pl.BlockSpec((tq, C_HD), lambda b, h, i: (b * nq + i, h)),
                  pl.BlockSpec((t, C_HD), lambda b, h, i: (b, h)),
                  pl.BlockSpec((t, C_HD), lambda b, h, i: (b, h))],
        out_specs=pl.BlockSpec((tq, C_HD), lambda b, h, i: (b * nq + i, h)),
        out_shape=jax.ShapeDtypeStruct((n, C_WIDTH), BF16),
        scratch_shapes=[pltpu.VMEM((tq, C_HD), F32)]
                       + [pltpu.VMEM((tq, span), F32)] * 2 + [pltpu.VMEM((tq, span), BF16)] * 2,
        compiler_params=_params("parallel", "parallel", "arbitrary"),
        name="sb_attention",
    )(bias, q, k, v)


def _decode_kernel(pt_ref, q_ref, kn_ref, vn_ref, *refs, n_new, n_pg):
    del pt_ref
    kc_refs, vc_refs = refs[:n_pg], refs[n_pg:2 * n_pg]
    bias_ref, o_ref, acc_sc, carry_sc = refs[2 * n_pg:]
    tk = ATT_TK
    R = SAMPLE_ROWS
    HR = C_HEADS * R
    j = pl.program_id(1)
    su = _strict_upper(tk)

    def weights(z, mask, carry_in):
        sp = _softplus(z)
        if mask is not None:
            sp = jnp.where(mask, sp, 0.0)
        hi = sp.astype(BF16)
        lo = (sp - hi.astype(F32)).astype(BF16)
        rest = _mm(hi, su) + _mm(lo, su)
        tot = rest[:, 0:1] + sp[:, 0:1]
        carries = [carry_in]
        for p in range(z.shape[0] // HR):
            carries.append(carries[-1] + tot[p * HR:(p + 1) * HR])
        a = jnp.exp(z - sp - rest - jnp.concatenate(carries[:-1], axis=0))
        if mask is not None:
            a = jnp.where(mask, a, 0.0)
        return a, carries[-1]

    @pl.when(j == 0)
    def _():
        pad = jnp.zeros((tk - R, C_HD), F32)
        zs, vals = [], []
        for h in range(C_HEADS):
            sl = slice(h * C_HD, (h + 1) * C_HD)
            kh = jnp.concatenate([kn_ref[:, sl], pad], axis=0).astype(BF16)
            vals.append(jnp.concatenate([vn_ref[:, sl], pad], axis=0).astype(BF16))
            zs.append(_nt(q_ref[:, sl].astype(BF16), kh))
        z = jnp.concatenate(zs, axis=0) * SB_SCALE + bias_ref[...]
        row = lax.broadcasted_iota(jnp.int32, z.shape, 0)
        col = lax.broadcasted_iota(jnp.int32, z.shape, 1)
        a, carry = weights(z, (col < row % R) & (col < n_new), jnp.zeros((HR, 1), F32))
        for h in range(C_HEADS):
            acc_sc[h * R:(h + 1) * R, :] = _mm(a[h * R:(h + 1) * R].astype(BF16), vals[h])
        carry_sc[...] = jnp.broadcast_to(carry, carry_sc.shape)

    zs = [[None] * C_HEADS for _ in range(n_pg)]
    vals = []
    for h in range(C_HEADS):
        sl = slice(h * C_HD, (h + 1) * C_HD)
        kh = jnp.concatenate([r[pl.ds(h, tk, stride=C_HEADS), :].astype(BF16) for r in kc_refs], axis=0)
        vals.append(jnp.concatenate([r[pl.ds(h, tk, stride=C_HEADS), :].astype(BF16) for r in vc_refs], axis=0))
        zh = _nt(q_ref[:, sl].astype(BF16), kh)
        for p in range(n_pg):
            zs[p][h] = zh[:, p * tk:(p + 1) * tk]
    z = jnp.concatenate([zs[p][h] for p in range(n_pg) for h in range(C_HEADS)], axis=0)
    z = z * SB_SCALE + jnp.concatenate([bias_ref[...]] * n_pg, axis=0)
    a, carry = weights(z, None, carry_sc[:, 0:1])
    for h in range(C_HEADS):
        ah = jnp.concatenate([a[(p * C_HEADS + h) * R:(p * C_HEADS + h + 1) * R] for p in range(n_pg)], axis=1)
        acc_sc[h * R:(h + 1) * R, :] += _mm(ah.astype(BF16), vals[h])
    carry_sc[...] = jnp.broadcast_to(carry, carry_sc.shape)

    @pl.when(j == pl.num_programs(1) - 1)
    def _():
        for h in range(C_HEADS):
            o_ref[:, h * C_HD:(h + 1) * C_HD] = acc_sc[h * R:(h + 1) * R, :]


def _decode_attn(q, k_new, v_new, cache_k, cache_v, page_table, bias_rows, layer, n_new):
    n_seq, n_pages = page_table.shape
    n_phys, n_att, page, heads, hd = cache_k.shape
    n_pg = DECODE_PAGES
    assert n_pages % n_pg == 0 and page == ATT_TK
    ck = cache_k.reshape(n_phys, n_att, page * heads, hd)
    cv = cache_v.reshape(n_phys, n_att, page * heads, hd)
    R = SAMPLE_ROWS
    row_map = lambda b, j, pt: (b, 0)

    def page_spec(p):
        return pl.BlockSpec((None, None, page * heads, hd),
                            lambda b, j, pt: (pt[b, n_pages - 1 - (j * n_pg + p)], layer, 0, 0))

    grid_spec = pltpu.PrefetchScalarGridSpec(
        num_scalar_prefetch=1,
        grid=(n_seq, n_pages // n_pg),
        in_specs=[pl.BlockSpec((R, C_WIDTH), row_map)] * 3
                 + [page_spec(p) for p in range(n_pg)] * 2
                 + [pl.BlockSpec((C_HEADS * R, LANES), lambda b, j, pt: (0, 0))],
        out_specs=pl.BlockSpec((R, C_WIDTH), row_map),
        scratch_shapes=[pltpu.VMEM((C_HEADS * R, C_HD), F32), pltpu.VMEM((C_HEADS * R, LANES), F32)],
    )
    return pl.pallas_call(
        functools.partial(_decode_kernel, n_new=n_new, n_pg=n_pg),
        grid_spec=grid_spec,
        out_shape=jax.ShapeDtypeStruct((n_seq * R, C_WIDTH), F32),
        compiler_params=_params("parallel", "arbitrary"),
        name="sb_decode",
    )(page_table, q, k_new, v_new, *([ck] * n_pg), *([cv] * n_pg), bias_rows)


def _trunk(x, mods, W, n_seq, rows_per_seq, tm, state, valid_len, past):
    n = x.shape[0]
    depth = W['w_up'].shape[0]
    n_att = W['w_qkv_c'].shape[0]
    block_rows = min(rows_per_seq, MIX_CHUNK * MIX_CHUNKS_PER_STEP)
    rec, rows, ks, vs = [], [], [], []
    if past is None:
        ks = vs = n_att
    for l in range(depth):
        sh1, sc1, g1, sh2, sc2, g2 = mods[l]
        if l % 2 == 0:
            e = l // 2
            p, gates = _proj(x, W['norm_w'][l, 0], sh1, sc1, W['w_in_ab'], e,
                             [(0, AB_MAIN, F32, None, None), (AB_MAIN, AB_MAIN + LANES, F32, None, None)],
                             tm, rows_per_seq, "proj_even")
            c0, m0 = _pack_state(*state(e))
            mixed, r, cfin, mfin = _mix(p, gates, W['b_if'][e], c0, m0, W['mh_norm_w'][e], W['sg_norm_w'][e],
                                        W['sg_w'][e], W['sg_b'][e], n_seq, block_rows, valid_len, "mix_even")
            rec.append(_unpack_state(cfin, mfin))
            rows.append(r)
        else:
            o = l // 2
            kcols, vcols = (C_WIDTH, 2 * C_WIDTH), (2 * C_WIDTH, 3 * C_WIDTH)
            if past is None:
                q, ks, vs, kb, vb = _proj(x, W['norm_w'][l, 0], sh1, sc1, W['w_qkv_c'], o,
                                          [(0, C_WIDTH, BF16, SB_SCALE * LOG2E, None),
                                           kcols + (F32, None, ks), vcols + (F32, None, vs),
                                           kcols + (BF16, None, None), vcols + (BF16, None, None)],
                                          tm, rows_per_seq, "proj_odd")
                mixed = _attn(q, kb, vb, W['b_sb'][o], n_seq, min(PROMPT_TQ, rows_per_seq))
            else:
                q, k, v = _proj(x, W['norm_w'][l, 0], sh1, sc1, W['w_qkv_c'], o,
                                [(0, C_WIDTH, F32, None, None), kcols + (F32, None, None),
                                 vcols + (F32, None, None)], tm, rows_per_seq, "proj_odd_s")
                bias_rows = jnp.broadcast_to(jnp.repeat(W['b_sb'][o], SAMPLE_ROWS)[:, None],
                                             (C_HEADS * SAMPLE_ROWS, LANES))
                mixed = _decode_attn(q, k, v, past[0], past[1], past[2], bias_rows, o, valid_len)
                ks.append(k)
                vs.append(v)
        x = _tail(mixed, W['w_out'], l, x, g1, W['norm_w'][l, 1], sh2, sc2, g2, W['w_up'], W['w_down'], l,
                  W['final_norm_w'], l == depth - 1, min(MLP_TM, n), MLP_TF, rows_per_seq, "tail")
    return x, rec, rows, ks, vs


def kernel(x_prompt, x_sample, state_mlstm_C, state_mlstm_n, state_mlstm_m, cache_k, cache_v, page_table, c_prompt, c_sample, w_ada, b_ada, norm_w, final_norm_w, w_in_ab, b_if, mh_norm_w, sg_norm_w, sg_w, sg_b, w_out_ab, w_qkv_c, b_sb, w_out_c, w_up, w_down):
    bp, t_p, d = x_prompt.shape
    bs, t_s, _ = x_sample.shape
    depth = w_ada.shape[0]
    n_rec = w_in_ab.shape[0]
    R = SAMPLE_ROWS
    assert t_s <= R and t_p % MIX_CHUNK == 0

    w_in = jnp.concatenate([w_in_ab[:, :, :AB_MAIN], w_in_ab[:, :, AB_MAIN:],
                            jnp.zeros((n_rec, d, LANES - 2 * A_HEADS), F32)], axis=-1).astype(BF16)
    causal = jnp.tril(jnp.ones((MIX_CHUNK, MIX_CHUNK), bool))
    W = dict(
        norm_w=norm_w, final_norm_w=final_norm_w, w_in_ab=w_in,
        b_if=jnp.concatenate([b_if, jnp.zeros((n_rec, LANES - 2 * A_HEADS), F32)], axis=-1).reshape(n_rec, 1, LANES),
        mh_norm_w=mh_norm_w.reshape(n_rec, 1, A_WIDTH), sg_norm_w=sg_norm_w.reshape(n_rec, 1, B_WIDTH),
        sg_w=jnp.where(causal, sg_w, 0.0).astype(BF16),
        sg_b=jnp.broadcast_to(sg_b[:, :, :MIX_CHUNK, None], (n_rec, B_GROUPS, MIX_CHUNK, LANES)),
        w_out=jnp.stack([(w_out_ab if l % 2 == 0 else w_out_c)[l // 2] for l in range(depth)]).astype(BF16),
        w_qkv_c=w_qkv_c.astype(BF16), b_sb=b_sb, w_up=w_up, w_down=w_down)

    n_c = bp + bs
    n_c_pad = -(-n_c // 8) * 8
    c_all = jnp.concatenate([c_prompt, c_sample, jnp.zeros((n_c_pad - n_c, d), F32)], axis=0)
    mod = _ada(c_all, w_ada, b_ada).reshape(depth, n_c_pad, N_MOD, d)
    mod_p = jnp.transpose(mod[:, :bp], (0, 2, 1, 3)).reshape(depth * N_MOD * bp, 1, d)
    mods_p = [[(mod_p, (l * N_MOD + i) * bp) for i in range(N_MOD)] for l in range(depth)]
    mod_s = jnp.transpose(jnp.repeat(mod[:, bp:bp + bs], R, axis=1), (0, 2, 1, 3))
    mod_s = mod_s.reshape(depth * N_MOD, bs * R, d)
    mods_s = [[(mod_s, l * N_MOD + i) for i in range(N_MOD)] for l in range(depth)]

    zero_state = (jnp.zeros((bp, A_HEADS, A_DK, A_DV), F32), jnp.zeros((bp, A_HEADS, A_DK), F32),
                  jnp.zeros((bp, A_HEADS), F32))
    yp, rec_p, _, ks_p, vs_p = _trunk(x_prompt.reshape(bp * t_p, d), mods_p, W, bp, t_p, 512,
                                      lambda e: zero_state, MIX_CHUNK, None)

    xs = jnp.pad(x_sample, ((0, 0), (0, R - t_s), (0, 0))).reshape(bs * R, d)
    ys, rec_s, rows_s, ks_s, vs_s = _trunk(
        xs, mods_s, W, bs, R, bs * R,
        lambda e: (state_mlstm_C[e], state_mlstm_n[e], state_mlstm_m[e]), t_s, (cache_k, cache_v, page_table))

    def unpad(a, tail):
        return a.reshape((bs, R) + tail)[:, :t_s]

    return (yp.reshape(bp, t_p, d), unpad(ys, (d,)),
            jnp.stack([r[0] for r in rec_p]), jnp.stack([r[1] for r in rec_p]), jnp.stack([r[2] for r in rec_p]),
            jnp.stack([r[0] for r in rec_s]), jnp.stack([r[1] for r in rec_s]), jnp.stack([r[2] for r in rec_s]),
            jnp.stack([unpad(r, (B_WIDTH,)) for r in rows_s]),
            ks_p.reshape(-1, bp, t_p, C_HEADS, C_HD), vs_p.reshape(-1, bp, t_p, C_HEADS, C_HD),
            jnp.stack([unpad(k, (C_HEADS, C_HD)) for k in ks_s]),
            jnp.stack([unpad(v, (C_HEADS, C_HD)) for v in vs_s]))
```

```python
import functools

import jax
import jax.numpy as jnp
from jax import lax
from jax.experimental import pallas as pl
from jax.experimental.pallas import tpu as pltpu

F32 = jnp.float32
BF16 = jnp.bfloat16

D_MODEL = 1024
A_HEADS = 4
A_DV = 128
A_DK = 64
A_WIDTH = A_HEADS * A_DV
A_QK_W = A_HEADS * A_DK
B_GROUPS = 4
B_WIDTH = 512
B_GROUP_DIM = 128
C_HEADS = 8
C_HD = 128
C_WIDTH = C_HEADS * C_HD
SB_SCALE = C_HD ** -0.5
D_FF = 4 * D_MODEL
N_MOD = 6
EPS = 1e-6
AB_MAIN = 2 * A_QK_W + 2 * A_WIDTH + 2 * B_WIDTH
MIX_CHUNK = 128
MIX_CHUNKS_PER_STEP = 4
ATT_TK = 128
PROMPT_TK = 256
PROMPT_TQ = 512
PROMPT_BLOCKS_PER_TRIP = 1
DECODE_PAGES = 16
MLP_TM = 1024
MLP_TF = 1024
LOG2E = 1.4426950408889634
EXP2_MAX = 126.0
SAMPLE_ROWS = 8
NEG = -1e30
LANES = 128
VMEM_LIMIT_V7X = 56 * 1024 * 1024


def _mm(a, b):
    return jnp.dot(a, b, preferred_element_type=F32)


def _nt(a, b):
    return lax.dot_general(a, b, (((1,), (1,)), ((), ())), preferred_element_type=F32)


def _tn(a, b):
    return lax.dot_general(a, b, (((0,), (0,)), ((), ())), preferred_element_type=F32)


def _softplus(z):
    return jnp.maximum(z, 0.0) + jnp.log1p(jnp.exp(-jnp.abs(z)))


def _rms(x, w):
    return x * lax.rsqrt(jnp.mean(x * x, axis=-1, keepdims=True) + EPS) * w


def _rms_mod(x, w, shift, scale):
    return _rms(x, w) * (1.0 + scale) + shift


def _split3(x):
    p1 = x.astype(BF16)
    r1 = x - p1.astype(F32)
    p2 = r1.astype(BF16)
    p3 = (r1 - p2.astype(F32)).astype(BF16)
    return p1, p2, p3


def _params(*sem):
    return pltpu.CompilerParams(dimension_semantics=sem, vmem_limit_bytes=VMEM_LIMIT_V7X)


def _mod_spec(mod, tm, rows_per_group):
    arr, base = mod
    g, r, d = arr.shape
    assert r == 1 or r == tm
    if r == 1:
        bpg = rows_per_group // tm
        return pl.BlockSpec((None, r, d), lambda i, *_: (base + i // bpg, 0, 0))
    return pl.BlockSpec((None, r, d), lambda i, *_: (base, 0, 0))


def _ada_kernel(c_ref, w_ref, b_ref, o_ref):
    c = c_ref[...]
    s = (c * jax.nn.sigmoid(c)).astype(BF16)
    o_ref[...] = _mm(s, w_ref[...].astype(BF16)) + b_ref[...]


def _ada(c_all, w_ada, b_ada):
    depth, d, n = w_ada.shape
    rows = c_all.shape[0]
    tn = 1536
    return pl.pallas_call(
        _ada_kernel,
        grid=(depth, n // tn),
        in_specs=[pl.BlockSpec((rows, d), lambda l, j: (0, 0)),
                  pl.BlockSpec((None, d, tn), lambda l, j: (l, 0, j)),
                  pl.BlockSpec((None, 1, tn), lambda l, j: (l, 0, j))],
        out_specs=pl.BlockSpec((None, rows, tn), lambda l, j: (l, 0, j)),
        out_shape=jax.ShapeDtypeStruct((depth, rows, n), F32),
        compiler_params=_params("parallel", "parallel"),
        name="ada_mod",
    )(c_all, w_ada, b_ada.reshape(depth, 1, n))


def _proj_kernel(x_ref, nw_ref, sh_ref, sc_ref, w_ref, *refs, outs):
    o_refs = refs[len(refs) - len(outs):]
    h = _rms_mod(x_ref[...], nw_ref[...], sh_ref[...], sc_ref[...]).astype(BF16)
    products = {}
    for o_ref, (a, b, _, mult, stacked) in zip(o_refs, outs):
        if (a, b) not in products:
            products[(a, b)] = _mm(h, w_ref[:, a:b])
        y = products[(a, b)]
        if mult is not None:
            y = y * mult
        if stacked == "update":
            o_ref[...] = y.reshape(o_ref.shape).astype(o_ref.dtype)
        elif stacked is not None:
            slot, n_slots = stacked
            for s in range(n_slots):
                val = y.reshape(o_ref.shape[1:]) if s == slot else jnp.zeros(o_ref.shape[1:], F32)
                o_ref[s] = val.astype(o_ref.dtype)
        else:
            o_ref[...] = y.astype(o_ref.dtype)


def _proj(x, nw, shift, scale, w_all, layer, outs, tm, rows_per_group, name):
    n, d = x.shape
    out_specs, out_shape, stacked_in, aliases, modes = [], [], [], {}, []
    for idx, (a, b, dt, _, stacked) in enumerate(outs):
        heads = (b - a) // LANES
        if stacked is None:
            out_specs.append(pl.BlockSpec((tm, b - a), lambda i: (i, 0)))
            out_shape.append(jax.ShapeDtypeStruct((n, b - a), dt))
            modes.append(None)
        elif isinstance(stacked, int):
            out_specs.append(pl.BlockSpec((stacked, tm, heads, LANES), lambda i: (0, i, 0, 0)))
            out_shape.append(jax.ShapeDtypeStruct((stacked, n, heads, LANES), dt))
            modes.append((layer, stacked))
        else:
            out_specs.append(pl.BlockSpec((None, tm) + stacked.shape[2:], lambda i: (layer, i, 0, 0)))
            out_shape.append(jax.ShapeDtypeStruct(stacked.shape, stacked.dtype))
            aliases[5 + len(stacked_in)] = idx
            stacked_in.append(stacked)
            modes.append("update")
    return pl.pallas_call(
        functools.partial(_proj_kernel, outs=tuple(o[:4] + (m,) for o, m in zip(outs, modes))),
        grid=(n // tm,),
        in_specs=[pl.BlockSpec((tm, d), lambda i: (i, 0)),
                  pl.BlockSpec((1, d), lambda i: (0, 0)),
                  _mod_spec(shift, tm, rows_per_group),
                  _mod_spec(scale, tm, rows_per_group),
                  pl.BlockSpec((None,) + w_all.shape[1:], lambda i: (layer, 0, 0))]
                 + [pl.BlockSpec(memory_space=pl.ANY)] * len(stacked_in),
        out_specs=out_specs,
        out_shape=out_shape,
        input_output_aliases=aliases,
        compiler_params=_params("parallel"),
        name=name,
    )(x, nw.reshape(1, d), shift[0], scale[0], w_all, *stacked_in)


def _tail_kernel(a_ref, wo_ref, x_ref, g1_ref, nw_ref, sh_ref, sc_ref, g2_ref, wu_ref, wd_ref, fnw_ref,
                 o_ref, h_sc, acc_sc, *, final):
    f = pl.program_id(1)

    @pl.when(f == 0)
    def _():
        x1 = x_ref[...] + g1_ref[...] * _mm(a_ref[...].astype(BF16), wo_ref[...])
        o_ref[...] = x1
        h_sc[...] = _rms_mod(x1, nw_ref[...], sh_ref[...], sc_ref[...]).astype(BF16)
        acc_sc[...] = jnp.zeros_like(acc_sc)

    up = _mm(h_sc[...], wu_ref[...].astype(BF16))
    act = jnp.square(jnp.maximum(up, 0.0)).astype(BF16)
    acc_sc[...] += _mm(act, wd_ref[...].astype(BF16))

    @pl.when(f == pl.num_programs(1) - 1)
    def _():
        y = o_ref[...] + g2_ref[...] * acc_sc[...]
        if final:
            y = _rms(y, fnw_ref[...])
        o_ref[...] = y


def _tail(a, w_out_all, mix_layer, x, gate1, nw, shift, scale, gate2, w_up_all, w_down_all, layer, fnw, final,
          tm, tf, rows_per_group, name):
    n, d = x.shape
    dff = w_up_all.shape[2]
    rows = lambda i, f: (i, 0)
    const = lambda i, f: (0, 0)
    return pl.pallas_call(
        functools.partial(_tail_kernel, final=final),
        grid=(n // tm, dff // tf),
        in_specs=[pl.BlockSpec((tm, a.shape[1]), rows),
                  pl.BlockSpec((None,) + w_out_all.shape[1:], lambda i, f: (mix_layer, 0, 0)),
                  pl.BlockSpec((tm, d), rows),
                  _mod_spec(gate1, tm, rows_per_group),
                  pl.BlockSpec((1, d), const),
                  _mod_spec(shift, tm, rows_per_group),
                  _mod_spec(scale, tm, rows_per_group),
                  _mod_spec(gate2, tm, rows_per_group),
                  pl.BlockSpec((None, d, tf), lambda i, f: (layer, 0, f)),
                  pl.BlockSpec((None, tf, d), lambda i, f: (layer, f, 0)),
                  pl.BlockSpec((1, d), const)],
        out_specs=pl.BlockSpec((tm, d), rows),
        out_shape=jax.ShapeDtypeStruct((n, d), F32),
        scratch_shapes=[pltpu.VMEM((tm, d), BF16), pltpu.VMEM((tm, d), F32)],
        compiler_params=_params("parallel", "arbitrary"),
        name=name,
    )(a, w_out_all, x, gate1[0], nw.reshape(1, d), shift[0], scale[0], gate2[0], w_up_all, w_down_all,
      fnw.reshape(1, d))


def _mix_kernel(p_ref, g_ref, bif_ref, c0_ref, m0_ref, mhw_ref, sgnw_ref, sgw_ref, sgb_ref,
                hz_ref, rows_ref, cfin_ref, mfin_ref, c_sc, m_sc, *, block_rows, valid_len):
    L = MIX_CHUNK
    c = pl.program_id(1)

    @pl.when(c == 0)
    def _():
        c_sc[...] = c0_ref[...]
        m_sc[...] = m0_ref[...]

    rowi = lax.broadcasted_iota(jnp.int32, (L, LANES), 0)
    lanei = lax.broadcasted_iota(jnp.int32, (L, LANES), 1)
    is_i = lanei < A_HEADS
    is_f = (lanei >= A_HEADS) & (lanei < 2 * A_HEADS)
    sq_r = lax.broadcasted_iota(jnp.int32, (L, L), 0)
    sq_c = lax.broadcasted_iota(jnp.int32, (L, L), 1)
    causal = sq_c <= sq_r
    tril = jnp.where(causal, 1.0, 0.0).astype(BF16)
    eye = jnp.where(sq_c == sq_r, 1.0, 0.0).astype(BF16)
    ones = jnp.ones((L, LANES), F32)
    sel_r = lax.broadcasted_iota(jnp.int32, (LANES, 2 * LANES), 0)
    sel_c = lax.broadcasted_iota(jnp.int32, (LANES, 2 * LANES), 1)
    sel = [jnp.where(((sel_r == h) & (sel_c < LANES)) | ((sel_r == A_HEADS + h) & (sel_c >= LANES)), 1.0,
                     0.0).astype(BF16) for h in range(A_HEADS)]

    def chunk(p, g):
        gb = g + bif_ref[...]
        x8 = jnp.where(is_i, gb, jnp.where(is_f, -_softplus(-gb), 0.0))
        if valid_len < L:
            x8 = jnp.where(rowi < valid_len, x8, jnp.where(is_i, NEG, 0.0))

        parts = _split3(x8)
        bc = _mm(tril, parts[0]) + _mm(tril, parts[1]) + _mm(tril, parts[2])
        yparts = _split3(jnp.where(is_i, x8, bc))
        yparts_t = [_nt(eye, q) for q in yparts]
        y_t = yparts_t[0] + yparts_t[1] + yparts_t[2]

        has = []
        for h in range(A_HEADS):
            hp, hl = h // 2, h % 2
            headmask = (lanei >= A_DK * hl) & (lanei < A_DK * (hl + 1))
            q2 = p[:, hp * LANES:(hp + 1) * LANES]
            k2 = p[:, A_QK_W + hp * LANES:A_QK_W + (hp + 1) * LANES]
            v = p[:, 2 * A_QK_W + h * A_DV:2 * A_QK_W + (h + 1) * A_DV]
            qh = jnp.where(headmask, q2, 0.0).astype(BF16)
            kh = jnp.where(headmask, k2 * (A_DK ** -0.5), 0.0)
            rep = _mm(yparts[0], sel[h]) + _mm(yparts[1], sel[h]) + _mm(yparts[2], sel[h])
            ig_rep, b_rep = rep[:, :LANES], rep[:, LANES:]
            m_prev = m_sc[h]

            gk = jnp.where(causal, y_t[h:h + 1, :] - y_t[A_HEADS + h:A_HEADS + h + 1, :], NEG)
            mx = jnp.maximum(m_prev, jnp.max(gk, axis=1, keepdims=True))
            m_t = b_rep + mx
            w_st = jnp.exp(m_prev - mx)
            pmat = _nt(qh, kh.astype(BF16)) * jnp.exp(gk - mx)
            v_ext = jnp.concatenate([v, ones], axis=1).astype(BF16)
            c_old = c_sc[h]
            num_ext = _mm(pmat.astype(BF16), v_ext) + jnp.concatenate([w_st, w_st], axis=1) * _mm(
                qh, c_old.astype(BF16))
            hh = num_ext[:, :A_DV] / jnp.maximum(jnp.abs(num_ext[:, A_DV:]), jnp.exp(-m_t))

            m_new = m_t[L - 1:L, :]
            b_end = b_rep[L - 1:L, :]
            w_k = jnp.exp(b_end - b_rep + ig_rep - m_new)
            decay = jnp.exp(b_end + m_prev - m_new)
            c_sc[h] = jnp.concatenate([decay, decay], axis=1) * c_old + _tn((kh * w_k).astype(BF16), v_ext)
            m_sc[h] = m_new

            o = p[:, 2 * A_QK_W + A_WIDTH + h * A_DV:2 * A_QK_W + A_WIDTH + (h + 1) * A_DV]
            has.append(_rms(hh, mhw_ref[:, h * A_DV:(h + 1) * A_DV]) * jax.nn.sigmoid(o))

        u0 = 2 * A_QK_W + 2 * A_WIDTH
        zbs, vsn = [], []
        for gi in range(B_GROUPS):
            u = jax.nn.gelu(p[:, u0 + gi * B_GROUP_DIM:u0 + (gi + 1) * B_GROUP_DIM])
            vs = jax.nn.gelu(p[:, u0 + B_WIDTH + gi * B_GROUP_DIM:u0 + B_WIDTH + (gi + 1) * B_GROUP_DIM])
            vs = _rms(vs, sgnw_ref[:, gi * B_GROUP_DIM:(gi + 1) * B_GROUP_DIM])
            s = _mm(sgw_ref[gi], vs.astype(BF16)) + sgb_ref[gi]
            zbs.append(u * s)
            vsn.append(vs)
        return jnp.concatenate(has + zbs, axis=1), jnp.concatenate(vsn, axis=1)

    if block_rows < L:
        pad = lambda a: jnp.concatenate([a, jnp.zeros((L - block_rows, a.shape[1]), F32)], axis=0)
        hz, vsn = chunk(pad(p_ref[...]), pad(g_ref[...]))
        hz_ref[...] = hz[:block_rows].astype(hz_ref.dtype)
        rows_ref[...] = vsn[:block_rows]
    else:
        for ci in range(block_rows // L):
            sl = slice(ci * L, (ci + 1) * L)
            hz, vsn = chunk(p_ref[sl, :], g_ref[sl, :])
            hz_ref[sl, :] = hz.astype(hz_ref.dtype)
            rows_ref[sl, :] = vsn

    @pl.when(c == pl.num_programs(1) - 1)
    def _():
        cfin_ref[...] = c_sc[...]
        mfin_ref[...] = m_sc[...]


def _mix(p, g, b_if, c0, m0, mh_norm_w, sg_norm_w, sg_w_tril, sg_b_t, n_seq, block_rows, valid_len, name):
    n = p.shape[0]
    nc = n // (n_seq * block_rows)
    row_map = lambda b, c: (b * nc + c, 0)
    const2 = lambda b, c: (0, 0)
    return pl.pallas_call(
        functools.partial(_mix_kernel, block_rows=block_rows, valid_len=valid_len),
        grid=(n_seq, nc),
        in_specs=[pl.BlockSpec((block_rows, AB_MAIN), row_map),
                  pl.BlockSpec((block_rows, LANES), row_map),
                  pl.BlockSpec((1, LANES), const2),
                  pl.BlockSpec((None, A_HEADS, LANES, 2 * LANES), lambda b, c: (b, 0, 0, 0)),
                  pl.BlockSpec((None, A_HEADS, 1, LANES), lambda b, c: (b, 0, 0, 0)),
                  pl.BlockSpec((1, A_WIDTH), const2),
                  pl.BlockSpec((1, B_WIDTH), const2),
                  pl.BlockSpec((B_GROUPS, MIX_CHUNK, MIX_CHUNK), lambda b, c: (0, 0, 0)),
                  pl.BlockSpec((B_GROUPS, MIX_CHUNK, LANES), lambda b, c: (0, 0, 0))],
        out_specs=[pl.BlockSpec((block_rows, A_WIDTH + B_WIDTH), row_map),
                   pl.BlockSpec((block_rows, B_WIDTH), row_map),
                   pl.BlockSpec((None, A_HEADS, LANES, 2 * LANES), lambda b, c: (b, 0, 0, 0)),
                   pl.BlockSpec((None, A_HEADS, 1, LANES), lambda b, c: (b, 0, 0, 0))],
        out_shape=[jax.ShapeDtypeStruct((n, A_WIDTH + B_WIDTH), BF16),
                   jax.ShapeDtypeStruct((n, B_WIDTH), F32),
                   jax.ShapeDtypeStruct((n_seq, A_HEADS, LANES, 2 * LANES), F32),
                   jax.ShapeDtypeStruct((n_seq, A_HEADS, 1, LANES), F32)],
        scratch_shapes=[pltpu.VMEM((A_HEADS, LANES, 2 * LANES), F32), pltpu.VMEM((A_HEADS, 1, LANES), F32)],
        compiler_params=_params("parallel", "arbitrary"),
        name=name,
    )(p, g, b_if, c0, m0, mh_norm_w, sg_norm_w, sg_w_tril, sg_b_t)


def _pack_state(C, n, m):
    b = C.shape[0]
    cn = jnp.concatenate([C, jnp.broadcast_to(n[..., None], C.shape[:3] + (LANES,))], axis=-1)
    z = jnp.zeros_like(cn)
    cn = cn.reshape(b, A_HEADS // 2, 2, A_DK, 2 * LANES)
    z = z.reshape(cn.shape)
    even = jnp.concatenate([cn[:, :, 0], z[:, :, 0]], axis=-2)
    odd = jnp.concatenate([z[:, :, 1], cn[:, :, 1]], axis=-2)
    c0 = jnp.stack([even, odd], axis=2).reshape(b, A_HEADS, 2 * A_DK, 2 * LANES)
    return c0, jnp.broadcast_to(m[:, :, None, None], (b, A_HEADS, 1, LANES))


def _unpack_state(cfin, mfin):
    b = cfin.shape[0]
    c5 = cfin.reshape(b, A_HEADS // 2, 2, 2, A_DK, 2 * LANES)
    cn = jnp.stack([c5[:, :, 0, 0], c5[:, :, 1, 1]], axis=2).reshape(b, A_HEADS, A_DK, 2 * LANES)
    return cn[..., :A_DV], cn[..., A_DV], mfin[:, :, 0, 0]


def _strict_upper(tk):
    r = lax.broadcasted_iota(jnp.int32, (tk, tk), 0)
    c = lax.broadcasted_iota(jnp.int32, (tk, tk), 1)
    return jnp.where(r > c, 1.0, 0.0).astype(BF16)


def _attn_kernel(bias_ref, q_ref, k_ref, v_ref, o_ref, acc_sc, z0_sc, z1_sc, a0_sc, a1_sc, *, tq):
    tk = PROMPT_TK
    nd = tq // tk
    nsub = PROMPT_BLOCKS_PER_TRIP
    span = nsub * tk
    assert nd % nsub == 0
    h = pl.program_id(1)
    qi = pl.program_id(2)
    ntrip = qi * (nd // nsub)
    bias2 = bias_ref[h] * LOG2E
    nsu = -_strict_upper(tk)

    def trip_start(t):
        return pl.multiple_of(jnp.maximum(qi * tq - (t + 1) * span, 0), tk)

    def weights(z2, ncarry, mask):
        e = jnp.exp2(jnp.minimum(z2, EXP2_MAX))
        sp = jnp.log(1.0 + e)
        if mask is not None:
            sp = jnp.where(mask, sp, 0.0)
        n = z2.shape[1] // tk
        parts = [None] * n
        for i in reversed(range(n)):
            sp_i = sp[:, i * tk:(i + 1) * tk]
            nrest = _mm(sp_i.astype(BF16), nsu)
            parts[i] = e[:, i * tk:(i + 1) * tk] * jnp.exp(nrest + ncarry - sp_i)
            ncarry = ncarry + nrest[:, 0:1] - sp_i[:, 0:1]
        a = parts[0] if n == 1 else jnp.concatenate(parts, axis=1)
        if mask is not None:
            a = jnp.where(mask, a, 0.0)
        return a.astype(BF16), ncarry

    z0_sc[...] = _nt(q_ref[...], k_ref[pl.ds(trip_start(0), span), :])
    a1_sc[...] = jnp.zeros_like(a1_sc)

    row = lax.broadcasted_iota(jnp.int32, (tk, tk), 0)
    col = lax.broadcasted_iota(jnp.int32, (tk, tk), 1)
    ncarries = []
    for rb in range(nd):
        q_rows = q_ref[rb * tk:(rb + 1) * tk, :]
        ncarry = jnp.zeros((tk, 1), F32)
        acc = jnp.zeros((tk, C_HD), F32)
        for sub in reversed(range(rb + 1)):
            start = pl.multiple_of(qi * tq + sub * tk, tk)
            z2 = _nt(q_rows, k_ref[pl.ds(start, tk), :]) + bias2
            a, ncarry = weights(z2, ncarry, col < row if sub == rb else None)
            acc = acc + _mm(a, v_ref[pl.ds(start, tk), :])
        acc_sc[rb * tk:(rb + 1) * tk, :] = acc
        ncarries.append(ncarry)

    def stage(t, z_cur, z_nxt, a_prev, a_cur, ncarry):
        acc_sc[...] += _mm(a_prev[...], v_ref[pl.ds(trip_start(t - 1), span), :])
        z_nxt[...] = _nt(q_ref[...], k_ref[pl.ds(trip_start(t + 1), span), :])
        a, ncarry = weights(z_cur[...] + bias2, ncarry, None)
        a_cur[...] = a
        return ncarry

    def two_trips(i, ncarry):
        ncarry = stage(2 * i, z0_sc, z1_sc, a1_sc, a0_sc, ncarry)
        return stage(2 * i + 1, z1_sc, z0_sc, a0_sc, a1_sc, ncarry)

    ncarry = lax.fori_loop(0, ntrip // 2, two_trips, jnp.concatenate(ncarries, axis=0))

    @pl.when(ntrip % 2 == 1)
    def _():
        stage(ntrip - 1, z0_sc, z1_sc, a1_sc, a0_sc, ncarry)
        acc_sc[...] += _mm(a0_sc[...], v_ref[pl.ds(trip_start(ntrip - 1), span), :])

    @pl.when(ntrip % 2 == 0)
    def _():
        acc_sc[...] += _mm(a1_sc[...], v_ref[pl.ds(trip_start(ntrip - 1), span), :])

    o_ref[...] = acc_sc[...].astype(o_ref.dtype)


def _attn(q, k, v, bias, n_seq, tq):
    n = q.shape[0]
    t = n // n_seq
    nq = t // tq
    span = PROMPT_BLOCKS_PER_TRIP * PROMPT_TK
    assert tq % span == 0
    return pl.pallas_call(
        functools.partial(_attn_kernel, tq=tq),
        grid=(n_seq, C_HEADS, nq),
        in_specs=[pl.BlockSpec(memory_space=pltpu.SMEM),
                  pl.BlockSpec((tq, C_HD), lambda b, h, i: (b * nq + i, h)),
                  pl.BlockSpec((t, C_HD), lambda b, h, i: (b, h)),
                  pl.BlockSpec((t, C_HD), lambda b, h, i: (b, h))],
        out_specs=pl.BlockSpec((tq, C_HD), lambda b, h, i: (b * nq + i, h)),
        out_shape=jax.ShapeDtypeStruct((n, C_WIDTH), BF16),
        scratch_shapes=[pltpu.VMEM((tq, C_HD), F32)]
                       + [pltpu.VMEM((tq, span), F32)] * 2 + [pltpu.VMEM((tq, span), BF16)] * 2,
        compiler_params=_params("parallel", "parallel", "arbitrary"),
        name="sb_attention",
    )(bias, q, k, v)


def _decode_kernel(pt_ref, q_ref, kn_ref, vn_ref, *refs, n_new, n_pg):
    del pt_ref
    kc_refs, vc_refs = refs[:n_pg], refs[n_pg:2 * n_pg]
    bias_ref, o_ref, acc_sc, carry_sc = refs[2 * n_pg:]
    tk = ATT_TK
    R = SAMPLE_ROWS
    HR = C_HEADS * R
    j = pl.program_id(1)
    su = _strict_upper(tk)

    def weights(z, mask, carry_in):
        sp = _softplus(z)
        if mask is not None:
            sp = jnp.where(mask, sp, 0.0)
        hi = sp.astype(BF16)
        lo = (sp - hi.astype(F32)).astype(BF16)
        rest = _mm(hi, su) + _mm(lo, su)
        tot = rest[:, 0:1] + sp[:, 0:1]
        carries = [carry_in]
        for p in range(z.shape[0] // HR):
            carries.append(carries[-1] + tot[p * HR:(p + 1) * HR])
        a = jnp.exp(z - sp - rest - jnp.concatenate(carries[:-1], axis=0))
        if mask is not None:
            a = jnp.where(mask, a, 0.0)
        return a, carries[-1]

    @pl.when(j == 0)
    def _():
        pad = jnp.zeros((tk - R, C_HD), F32)
        zs, vals = [], []
        for h in range(C_HEADS):
            sl = slice(h * C_HD, (h + 1) * C_HD)
            kh = jnp.concatenate([kn_ref[:, sl], pad], axis=0).astype(BF16)
            vals.append(jnp.concatenate([vn_ref[:, sl], pad], axis=0).astype(BF16))
            zs.append(_nt(q_ref[:, sl].astype(BF16), kh))
        z = jnp.concatenate(zs, axis=0) * SB_SCALE + bias_ref[...]
        row = lax.broadcasted_iota(jnp.int32, z.shape, 0)
        col = lax.broadcasted_iota(jnp.int32, z.shape, 1)
        a, carry = weights(z, (col < row % R) & (col < n_new), jnp.zeros((HR, 1), F32))
        for h in range(C_HEADS):
            acc_sc[h * R:(h + 1) * R, :] = _mm(a[h * R:(h + 1) * R].astype(BF16), vals[h])
        carry_sc[...] = jnp.broadcast_to(carry, carry_sc.shape)

    zs = [[None] * C_HEADS for _ in range(n_pg)]
    vals = []
    for h in range(C_HEADS):
        sl = slice(h * C_HD, (h + 1) * C_HD)
        kh = jnp.concatenate([r[pl.ds(h, tk, stride=C_HEADS), :].astype(BF16) for r in kc_refs], axis=0)
        vals.append(jnp.concatenate([r[pl.ds(h, tk, stride=C_HEADS), :].astype(BF16) for r in vc_refs], axis=0))
        zh = _nt(q_ref[:, sl].astype(BF16), kh)
        for p in range(n_pg):
            zs[p][h] = zh[:, p * tk:(p + 1) * tk]
    z = jnp.concatenate([zs[p][h] for p in range(n_pg) for h in range(C_HEADS)], axis=0)
    z = z * SB_SCALE + jnp.concatenate([bias_ref[...]] * n_pg, axis=0)
    a, carry = weights(z, None, carry_sc[:, 0:1])
    for h in range(C_HEADS):
        ah = jnp.concatenate([a[(p * C_HEADS + h) * R:(p * C_HEADS + h + 1) * R] for p in range(n_pg)], axis=1)
        acc_sc[h * R:(h + 1) * R, :] += _mm(ah.astype(BF16), vals[h])
    carry_sc[...] = jnp.broadcast_to(carry, carry_sc.shape)

    @pl.when(j == pl.num_programs(1) - 1)
    def _():
        for h in range(C_HEADS):
            o_ref[:, h * C_HD:(h + 1) * C_HD] = acc_sc[h * R:(h + 1) * R, :]


def _decode_attn(q, k_new, v_new, cache_k, cache_v, page_table, bias_rows, layer, n_new):
    n_seq, n_pages = page_table.shape
    n_phys, n_att, page, heads, hd = cache_k.shape
    n_pg = DECODE_PAGES
    assert n_pages % n_pg == 0 and page == ATT_TK
    ck = cache_k.reshape(n_phys, n_att, page * heads, hd)
    cv = cache_v.reshape(n_phys, n_att, page * heads, hd)
    R = SAMPLE_ROWS
    row_map = lambda b, j, pt: (b, 0)

    def page_spec(p):
        return pl.BlockSpec((None, None, page * heads, hd),
                            lambda b, j, pt: (pt[b, n_pages - 1 - (j * n_pg + p)], layer, 0, 0))

    grid_spec = pltpu.PrefetchScalarGridSpec(
        num_scalar_prefetch=1,
        grid=(n_seq, n_pages // n_pg),
        in_specs=[pl.BlockSpec((R, C_WIDTH), row_map)] * 3
                 + [page_spec(p) for p in range(n_pg)] * 2
                 + [pl.BlockSpec((C_HEADS * R, LANES), lambda b, j, pt: (0, 0))],
        out_specs=pl.BlockSpec((R, C_WIDTH), row_map),
        scratch_shapes=[pltpu.VMEM((C_HEADS * R, C_HD), F32), pltpu.VMEM((C_HEADS * R, LANES), F32)],
    )
    return pl.pallas_call(
        functools.partial(_decode_kernel, n_new=n_new, n_pg=n_pg),
        grid_spec=grid_spec,
        out_shape=jax.ShapeDtypeStruct((n_seq * R, C_WIDTH), F32),
        compiler_params=_params("parallel", "arbitrary"),
        name="sb_decode",
    )(page_table, q, k_new, v_new, *([ck] * n_pg), *([cv] * n_pg), bias_rows)


def _trunk(x, mods, W, n_seq, rows_per_seq, tm, state, valid_len, past):
    n = x.shape[0]
    depth = W['w_up'].shape[0]
    n_att = W['w_qkv_c'].shape[0]
    block_rows = min(rows_per_seq, MIX_CHUNK * MIX_CHUNKS_PER_STEP)
    rec, rows, ks, vs = [], [], [], []
    if past is None:
        ks = vs = n_att
    for l in range(depth):
        sh1, sc1, g1, sh2, sc2, g2 = mods[l]
        if l % 2 == 0:
            e = l // 2
            p, gates = _proj(x, W['norm_w'][l, 0], sh1, sc1, W['w_in_ab'], e,
                             [(0, AB_MAIN, F32, None, None), (AB_MAIN, AB_MAIN + LANES, F32, None, None)],
                             tm, rows_per_seq, "proj_even")
            c0, m0 = _pack_state(*state(e))
            mixed, r, cfin, mfin = _mix(p, gates, W['b_if'][e], c0, m0, W['mh_norm_w'][e], W['sg_norm_w'][e],
                                        W['sg_w'][e], W['sg_b'][e], n_seq, block_rows, valid_len, "mix_even")
            rec.append(_unpack_state(cfin, mfin))
            rows.append(r)
        else:
            o = l // 2
            kcols, vcols = (C_WIDTH, 2 * C_WIDTH), (2 * C_WIDTH, 3 * C_WIDTH)
            if past is None:
                q, ks, vs, kb, vb = _proj(x, W['norm_w'][l, 0], sh1, sc1, W['w_qkv_c'], o,
                                          [(0, C_WIDTH, BF16, SB_SCALE * LOG2E, None),
                                           kcols + (F32, None, ks), vcols + (F32, None, vs),
                                           kcols + (BF16, None, None), vcols + (BF16, None, None)],
                                          tm, rows_per_seq, "proj_odd")
                mixed = _attn(q, kb, vb, W['b_sb'][o], n_seq, min(PROMPT_TQ, rows_per_seq))
            else:
                q, k, v = _proj(x, W['norm_w'][l, 0], sh1, sc1, W['w_qkv_c'], o,
                                [(0, C_WIDTH, F32, None, None), kcols + (F32, None, None),
                                 vcols + (F32, None, None)], tm, rows_per_seq, "proj_odd_s")
                bias_rows = jnp.broadcast_to(jnp.repeat(W['b_sb'][o], SAMPLE_ROWS)[:, None],
                                             (C_HEADS * SAMPLE_ROWS, LANES))
                mixed = _decode_attn(q, k, v, past[0], past[1], past[2], bias_rows, o, valid_len)
                ks.append(k)
                vs.append(v)
        x = _tail(mixed, W['w_out'], l, x, g1, W['norm_w'][l, 1], sh2, sc2, g2, W['w_up'], W['w_down'], l,
                  W['final_norm_w'], l == depth - 1, min(MLP_TM, n), MLP_TF, rows_per_seq, "tail")
    return x, rec, rows, ks, vs


def kernel(x_prompt, x_sample, state_mlstm_C, state_mlstm_n, state_mlstm_m, cache_k, cache_v, page_table, c_prompt, c_sample, w_ada, b_ada, norm_w, final_norm_w, w_in_ab, b_if, mh_norm_w, sg_norm_w, sg_w, sg_b, w_out_ab, w_qkv_c, b_sb, w_out_c, w_up, w_down):
    bp, t_p, d = x_prompt.shape
    bs, t_s, _ = x_sample.shape
    depth = w_ada.shape[0]
    n_rec = w_in_ab.shape[0]
    R = SAMPLE_ROWS
    assert t_s <= R and t_p % MIX_CHUNK == 0

    w_in = jnp.concatenate([w_in_ab[:, :, :AB_MAIN], w_in_ab[:, :, AB_MAIN:],
                            jnp.zeros((n_rec, d, LANES - 2 * A_HEADS), F32)], axis=-1).astype(BF16)
    causal = jnp.tril(jnp.ones((MIX_CHUNK, MIX_CHUNK), bool))
    W = dict(
        norm_w=norm_w, final_norm_w=final_norm_w, w_in_ab=w_in,
        b_if=jnp.concatenate([b_if, jnp.zeros((n_rec, LANES - 2 * A_HEADS), F32)], axis=-1).reshape(n_rec, 1, LANES),
        mh_norm_w=mh_norm_w.reshape(n_rec, 1, A_WIDTH), sg_norm_w=sg_norm_w.reshape(n_rec, 1, B_WIDTH),
        sg_w=jnp.where(causal, sg_w, 0.0).astype(BF16),
        sg_b=jnp.broadcast_to(sg_b[:, :, :MIX_CHUNK, None], (n_rec, B_GROUPS, MIX_CHUNK, LANES)),
        w_out=jnp.stack([(w_out_ab if l % 2 == 0 else w_out_c)[l // 2] for l in range(depth)]).astype(BF16),
        w_qkv_c=w_qkv_c.astype(BF16), b_sb=b_sb, w_up=w_up.astype(BF16), w_down=w_down.astype(BF16))

    n_c = bp + bs
    n_c_pad = -(-n_c // 8) * 8
    c_all = jnp.concatenate([c_prompt, c_sample, jnp.zeros((n_c_pad - n_c, d), F32)], axis=0)
    mod = _ada(c_all, w_ada, b_ada).reshape(depth, n_c_pad, N_MOD, d)
    mod_p = jnp.transpose(mod[:, :bp], (0, 2, 1, 3)).reshape(depth * N_MOD * bp, 1, d)
    mods_p = [[(mod_p, (l * N_MOD + i) * bp) for i in range(N_MOD)] for l in range(depth)]
    mod_s = jnp.transpose(jnp.repeat(mod[:, bp:bp + bs], R, axis=1), (0, 2, 1, 3))
    mod_s = mod_s.reshape(depth * N_MOD, bs * R, d)
    mods_s = [[(mod_s, l * N_MOD + i) for i in range(N_MOD)] for l in range(depth)]

    zero_state = (jnp.zeros((bp, A_HEADS, A_DK, A_DV), F32), jnp.zeros((bp, A_HEADS, A_DK), F32),
                  jnp.zeros((bp, A_HEADS), F32))
    yp, rec_p, _, ks_p, vs_p = _trunk(x_prompt.reshape(bp * t_p, d), mods_p, W, bp, t_p, 512,
                                      lambda e: zero_state, MIX_CHUNK, None)

    xs = jnp.pad(x_sample, ((0, 0), (0, R - t_s), (0, 0))).reshape(bs * R, d)
    ys, rec_s, rows_s, ks_s, vs_s = _trunk(
        xs, mods_s, W, bs, R, bs * R,
        lambda e: (state_mlstm_C[e], state_mlstm_n[e], state_mlstm_m[e]), t_s, (cache_k, cache_v, page_table))

    def unpad(a, tail):
        return a.reshape((bs, R) + tail)[:, :t_s]

    return (yp.reshape(bp, t_p, d), unpad(ys, (d,)),
            jnp.stack([r[0] for r in rec_p]), jnp.stack([r[1] for r in rec_p]), jnp.stack([r[2] for r in rec_p]),
            jnp.stack([r[0] for r in rec_s]), jnp.stack([r[1] for r in rec_s]), jnp.stack([r[2] for r in rec_s]),
            jnp.stack([unpad(r, (B_WIDTH,)) for r in rows_s]),
            ks_p.reshape(-1, bp, t_p, C_HEADS, C_HD), vs_p.reshape(-1, bp, t_p, C_HEADS, C_HD),
            jnp.stack([unpad(k, (C_HEADS, C_HD)) for k in ks_s]),
            jnp.stack([unpad(v, (C_HEADS, C_HD)) for v in vs_s]))
```
